```python
import math
import jax, jax.numpy as jnp
from jax import lax
import numpy as np

D_MODEL = 1024
BATCH = 8
SEQ = 2048
DEPTH = 2

GRID_W = 64
CTX_LEN = 256
EPS = 1e-6
N_MOD = 6

N_HEADS = 8
N_KV_HEADS = 2
GQA = N_HEADS // N_KV_HEADS
HEAD_DIM = 64
WINDOW = 128
ATT_BLOCK = 128
ROPE_BASE = 10000.0
ROPE_AXIS_PAIRS = HEAD_DIM // 4

S5_GROUPS = 16
S5_GROUP_CH = 16
S5_STATE = 64
S5_W = S5_GROUPS * S5_GROUP_CH

RET_HEADS = 4
RET_HEAD_DIM = 64
RET_W = RET_HEADS * RET_HEAD_DIM
RET_CHUNK = 128
RET_ROPE_BASE = 10000.0

ATT_Q_W = N_HEADS * HEAD_DIM
ATT_KV_W = N_KV_HEADS * HEAD_DIM
MIX_W = ATT_Q_W + S5_W + RET_W
PROJ_W = ATT_Q_W + 2 * ATT_KV_W + S5_W + 4 * RET_W

PEER_HEADS = 8
PEER_N_KEYS = 128
PEER_EXPERTS = PEER_N_KEYS * PEER_N_KEYS
PEER_QUERY_DIM = 128
PEER_HALF = PEER_QUERY_DIM // 2
PEER_TOPK = 16
PEER_TOKEN_BLOCK = 128

kernel_name = "hybrid_flow_backbone_attn_s5_retention_peer"

F32 = jnp.float32


def rms_norm(x, w):
    xf = x.astype(F32)
    y = xf * lax.rsqrt(jnp.mean(xf * xf, axis=-1, keepdims=True) + EPS)
    return (y * w).astype(x.dtype)


def modulate(x, shift, scale):
    return x * (1.0 + scale) + shift


def adaln_params(cvec, w, b):
    m = jax.nn.silu(cvec) @ w + b
    return m.reshape(cvec.shape[:-1] + (N_MOD, D_MODEL))


def split_projection(z):
    sizes = (ATT_Q_W, ATT_KV_W, ATT_KV_W, S5_W, RET_W, RET_W, RET_W, RET_W)
    offsets = [sum(sizes[:i]) for i in range(1, len(sizes))]
    return jnp.split(z, offsets, axis=-1)


def axial_rope_tables(rows):
    t = jnp.arange(rows * GRID_W)
    r = (t // GRID_W).astype(F32)
    col = (t % GRID_W).astype(F32)
    inv = ROPE_BASE ** (-jnp.arange(ROPE_AXIS_PAIRS, dtype=F32) / ROPE_AXIS_PAIRS)
    ang = jnp.concatenate([r[:, None] * inv, col[:, None] * inv], axis=-1)
    return jnp.cos(ang), jnp.sin(ang)


def flat_rope_tables(length, dim):
    t = jnp.arange(length, dtype=F32)
    inv = RET_ROPE_BASE ** (-jnp.arange(dim // 2, dtype=F32) / (dim // 2))
    ang = t[:, None] * inv
    return jnp.cos(ang), jnp.sin(ang)


def apply_rope(x, cos, sin):
    half = x.shape[-1] // 2
    bshape = (1, cos.shape[0]) + (1,) * (x.ndim - 3) + (half,)
    cos = cos.reshape(bshape)
    sin = sin.reshape(bshape)
    xf = x.astype(F32)
    x1, x2 = xf[..., :half], xf[..., half:]
    return jnp.concatenate([x1 * cos - x2 * sin, x1 * sin + x2 * cos], axis=-1).astype(x.dtype)


def window_attention(q, k, v, kc, vc, sink_logit):
    b, n = q.shape[:2]
    nb = n // ATT_BLOCK
    qb = q.reshape(b, nb, ATT_BLOCK, N_KV_HEADS, GQA, HEAD_DIM)

    def band(t):
        tp = jnp.pad(t, ((0, 0), (ATT_BLOCK, ATT_BLOCK), (0, 0), (0, 0)))
        tp = tp.reshape(b, nb + 2, ATT_BLOCK, N_KV_HEADS, HEAD_DIM)
        return jnp.concatenate([tp[:, :-2], tp[:, 1:-1], tp[:, 2:]], axis=2)

    kw, vw = band(k), band(v)
    nw = 3 * ATT_BLOCK
    s_win = jnp.einsum('bnqhgd,bnkhd->bnhgqk', qb, kw).astype(F32)
    qi = jnp.arange(ATT_BLOCK)[:, None]
    kj = jnp.arange(nw)[None, :]
    key_pos = jnp.arange(nb)[:, None, None] * ATT_BLOCK - ATT_BLOCK + kj[None]
    valid = (jnp.abs(kj - ATT_BLOCK - qi) <= WINDOW)[None] & (key_pos >= 0) & (key_pos < n)
    s_win = jnp.where(valid[None, :, None, None], s_win, -jnp.inf)
    s_ctx = jnp.einsum('bnqhgd,bchd->bnhgqc', qb, kc).astype(F32)
    s_sink = jnp.broadcast_to(sink_logit[None, None, :, :, None, None], s_ctx.shape[:-1] + (1,))
    p = jax.nn.softmax(jnp.concatenate([s_win, s_ctx, s_sink], axis=-1), axis=-1)
    p_win = p[..., :nw].astype(v.dtype)
    p_ctx = p[..., nw:-1].astype(v.dtype)
    o = (jnp.einsum('bnhgqk,bnkhd->bnqhgd', p_win, vw)
         + jnp.einsum('bnhgqc,bchd->bnqhgd', p_ctx, vc))
    return o.reshape(b, n, ATT_Q_W)


def context_attention(qc, kc, vc, sink_logit):
    b, n = qc.shape[:2]
    s = jnp.einsum('bqhgd,bkhd->bhgqk', qc, kc).astype(F32)
    s_sink = jnp.broadcast_to(sink_logit[None, :, :, None, None], s.shape[:-1] + (1,))
    p = jax.nn.softmax(jnp.concatenate([s, s_sink], axis=-1), axis=-1)[..., :-1].astype(vc.dtype)
    o = jnp.einsum('bhgqk,bkhd->bqhgd', p, vc)
    return o.reshape(b, n, ATT_Q_W)


def attention_group(zq, zk, zv, zqc, zkc, zvc, q_norm, k_norm, sink, cos, sin, need_ctx):
    scale = HEAD_DIM ** -0.5

    def prep(a, bk, cv):
        bsz, n = a.shape[:2]
        q = rms_norm(a.reshape(bsz, n, N_KV_HEADS, GQA, HEAD_DIM), q_norm)
        k = rms_norm(bk.reshape(bsz, n, N_KV_HEADS, HEAD_DIM), k_norm)
        v = cv.reshape(bsz, n, N_KV_HEADS, HEAD_DIM)
        return q, k, v

    q, k, v = prep(zq, zk, zv)
    q = apply_rope(q, cos, sin) * scale
    k = apply_rope(k, cos, sin)
    qc, kc, vc = prep(zqc, zkc, zvc)
    qc = qc * scale
    sink_logit = sink.astype(F32).reshape(N_KV_HEADS, GQA)
    out = window_attention(q, k, v, kc, vc, sink_logit)
    out_c = context_attention(qc, kc, vc, sink_logit) if need_ctx else None
    return out, out_c


def _ssm_combine(e_i, e_j):
    a_i, b_i = e_i
    a_j, b_j = e_j
    return a_j * a_i, a_j * b_i + b_j


def s5_discretize(a_re, a_im, log_dt, b_re, b_im):
    lam = lax.complex(a_re.astype(F32), a_im.astype(F32))
    dt = jnp.exp(log_dt.astype(F32))
    a_bar = jnp.exp(lam * dt)
    b = lax.complex(b_re.astype(F32), b_im.astype(F32))
    b_bar = ((a_bar - 1.0) / lam)[..., None] * b
    return a_bar, b_bar


def s5_scan(u, a_bar, b_bar, h0, reverse):
    bu = jnp.einsum('gph,blgh->blgp', b_bar, u.astype(jnp.complex64))
    idx = -1 if reverse else 0
    bu = bu.at[:, idx].add(a_bar * h0)
    a = jnp.broadcast_to(a_bar, bu.shape)
    _, hs = lax.associative_scan(_ssm_combine, (a, bu), reverse=reverse, axis=1)
    return hs


def s5_output(y, u, d, glu_w, glu_b):
    bsz, n = y.shape[:2]
    y = y.reshape(bsz, n, S5_W) + d * u
    y = jax.nn.gelu(y)
    return y * jax.nn.sigmoid(y @ glu_w + glu_b)


def s5_group(zu, zuc, a_re, a_im, log_dt, b_re, b_im, c_re, c_im, d, glu_w, glu_b, need_ctx):
    bsz, n = zu.shape[:2]
    nc = zuc.shape[1]
    u = zu.astype(F32)
    uc = zuc.astype(F32)
    ug = u.reshape(bsz, n, S5_GROUPS, S5_GROUP_CH)
    ucg = uc.reshape(bsz, nc, S5_GROUPS, S5_GROUP_CH)
    h0 = jnp.zeros((bsz, S5_GROUPS, S5_STATE), jnp.complex64)
    ys, ycs = [], []
    for direction in range(2):
        rev = direction == 1
        a_bar, b_bar = s5_discretize(a_re[direction], a_im[direction], log_dt[direction],
                                     b_re[direction], b_im[direction])
        cmat = lax.complex(c_re[direction].astype(F32), c_im[direction].astype(F32))
        hc = s5_scan(ucg, a_bar, b_bar, h0, rev)
        h_ctx_final = hc[:, 0] if rev else hc[:, -1]
        hl = s5_scan(ug, a_bar, b_bar, h_ctx_final, rev)
        ys.append(jnp.real(jnp.einsum('ghp,blgp->blgh', cmat, hl)))
        if need_ctx:
            ycs.append(jnp.real(jnp.einsum('ghp,blgp->blgh', cmat, hc)))
    out = s5_output(ys[0] + ys[1], u, d, glu_w, glu_b).astype(zu.dtype)
    out_c = s5_output(ycs[0] + ycs[1], uc, d, glu_w, glu_b).astype(zu.dtype) if need_ctx else None
    return out, out_c


def retention_chunkwise(q, k, v, log_gamma, s0):
    b, h, n, d = q.shape
    nc = n // RET_CHUNK
    qc = q.reshape(b, h, nc, RET_CHUNK, d)
    kc = k.reshape(b, h, nc, RET_CHUNK, d)
    vc = v.reshape(b, h, nc, RET_CHUNK, d)
    pos = jnp.arange(RET_CHUNK, dtype=F32)
    lg = log_gamma[:, None]
    rel = pos[:, None] - pos[None, :]
    intra_decay = jnp.where(rel >= 0, jnp.exp(lg[:, :, None] * jnp.maximum(rel, 0.0)), 0.0)
    scores = jnp.einsum('bhncd,bhnmd->bhncm', qc, kc) * intra_decay[None, :, None]
    intra = jnp.einsum('bhncm,bhnme->bhnce', scores, vc)
    k_w = jnp.exp(lg * (RET_CHUNK - 1 - pos))[None, :, None, :, None]
    kv = jnp.einsum('bhnmd,bhnme->bhnde', kc * k_w, vc)
    q_w = jnp.exp(lg * (pos + 1.0))[None, :, None, :, None]
    chunk_decay = jnp.exp(log_gamma * RET_CHUNK)[None, :, None, None]

    def step(s, kv_n):
        return chunk_decay * s + kv_n, s

    s_final, s_prev = lax.scan(step, s0, jnp.moveaxis(kv, 2, 0))
    cross = jnp.einsum('bhncd,nbhde->bhnce', qc * q_w, s_prev)
    return (intra + cross).reshape(b, h, n, d), s_final


def retention_output(o, zg):
    o = o * lax.rsqrt(jnp.mean(o * o, axis=-1, keepdims=True) + EPS)
    b, _, n, _ = o.shape
    o = jnp.swapaxes(o, 1, 2).reshape(b, n, RET_W)
    return (o * jax.nn.silu(zg.astype(F32))).astype(zg.dtype)


def retention_group(zq, zk, zv, zg, zqc, zkc, zvc, zgc, decay_logit, cos, sin, need_ctx):
    log_gamma = jax.nn.log_sigmoid(decay_logit.astype(F32))
    scale = RET_HEAD_DIM ** -0.5

    def heads(z):
        b, n = z.shape[:2]
        return z.astype(F32).reshape(b, n, RET_HEADS, RET_HEAD_DIM)

    q = jnp.swapaxes(apply_rope(heads(zq), cos, sin) * scale, 1, 2)
    k = jnp.swapaxes(apply_rope(heads(zk), cos, sin), 1, 2)
    v = jnp.swapaxes(heads(zv), 1, 2)
    qc = jnp.swapaxes(heads(zqc) * scale, 1, 2)
    kc = jnp.swapaxes(heads(zkc), 1, 2)
    vc = jnp.swapaxes(heads(zvc), 1, 2)
    s0 = jnp.zeros((q.shape[0], RET_HEADS, RET_HEAD_DIM, RET_HEAD_DIM), F32)

    def flip(t):
        return jnp.flip(t, axis=2)

    oc_f, sc_f = retention_chunkwise(qc, kc, vc, log_gamma[0], s0)
    o_f, _ = retention_chunkwise(q, k, v, log_gamma[0], sc_f)
    oc_b, sc_b = retention_chunkwise(flip(qc), flip(kc), flip(vc), log_gamma[1], s0)
    o_b, _ = retention_chunkwise(flip(q), flip(k), flip(v), log_gamma[1], sc_b)
    out = retention_output(o_f + flip(o_b), zg)
    out_c = retention_output(oc_f + flip(oc_b), zgc) if need_ctx else None
    return out, out_c


def peer_ffn(h, wq, sub_keys, u, v):
    shape = h.shape
    tok = h.reshape(-1, D_MODEL)
    blocks = tok.reshape(tok.shape[0] // PEER_TOKEN_BLOCK, PEER_TOKEN_BLOCK, D_MODEL)

    def block_fn(hb):
        q = (hb @ wq).reshape(PEER_TOKEN_BLOCK, PEER_HEADS, 2, PEER_HALF)
        s1 = jnp.einsum('thd,hkd->thk', q[:, :, 0], sub_keys[0]).astype(F32)
        s2 = jnp.einsum('thd,hkd->thk', q[:, :, 1], sub_keys[1]).astype(F32)
        t1, i1 = lax.top_k(s1, PEER_TOPK)
        t2, i2 = lax.top_k(s2, PEER_TOPK)
        cand = (t1[..., :, None] + t2[..., None, :]).reshape(PEER_TOKEN_BLOCK, PEER_HEADS, PEER_TOPK * PEER_TOPK)
        cidx = (i1[..., :, None] * PEER_N_KEYS + i2[..., None, :]).reshape(PEER_TOKEN_BLOCK, PEER_HEADS, PEER_TOPK * PEER_TOPK)
        best, sel = lax.top_k(cand, PEER_TOPK)
        eidx = jnp.take_along_axis(cidx, sel, axis=-1)
        gate = jax.nn.softmax(best, axis=-1)
        act = jax.nn.gelu(jnp.einsum('thkd,td->thk', u[eidx], hb).astype(F32))
        w = (gate * act).astype(hb.dtype)
        return jnp.einsum('thk,thkd->td', w, v[eidx])

    return lax.map(block_fn, blocks).reshape(shape)


def setup_inputs(seed: int = 0) -> dict:
    key = jax.random.key(seed)
    ks = jax.random.split(key, 32)

    def nrm(k, shape, scale):
        return jax.random.normal(k, shape, F32) * scale

    L = DEPTH
    s5_shape = (L, 2, S5_GROUPS, S5_STATE)
    n_idx = jnp.arange(S5_STATE, dtype=F32)
    ret_base = jnp.log(2.0 ** (5.0 + jnp.arange(RET_HEADS, dtype=F32)) - 1.0)
    return {
        "x": nrm(ks[0], (BATCH, SEQ, D_MODEL), 1.0),
        "c": nrm(ks[1], (BATCH, D_MODEL), 1.0),
        "ctx": nrm(ks[2], (BATCH, CTX_LEN, D_MODEL), 1.0),
        "c_ctx": nrm(ks[3], (D_MODEL,), 1.0),
        "ada_w": nrm(ks[4], (L, D_MODEL, N_MOD * D_MODEL), 0.5 * D_MODEL ** -0.5),
        "ada_b": nrm(ks[5], (L, N_MOD * D_MODEL), 0.02),
        "norm1_w": 1.0 + nrm(ks[6], (L, D_MODEL), 0.02),
        "norm2_w": 1.0 + nrm(ks[7], (L, D_MODEL), 0.02),
        "w_in": nrm(ks[8], (L, D_MODEL, PROJ_W), D_MODEL ** -0.5),
        "w_out": nrm(ks[9], (L, MIX_W, D_MODEL), MIX_W ** -0.5),
        "q_norm_w": 1.0 + nrm(ks[10], (L, HEAD_DIM), 0.02),
        "k_norm_w": 1.0 + nrm(ks[11], (L, HEAD_DIM), 0.02),
        "attn_sink": nrm(ks[12], (L, N_HEADS), 0.5),
        "s5_a_re": -0.5 + nrm(ks[13], s5_shape, 0.01),
        "s5_a_im": math.pi * n_idx + nrm(ks[14], s5_shape, 0.01),
        "s5_log_dt": jax.random.uniform(ks[15], s5_shape, F32, math.log(1e-3), math.log(1e-1)),
        "s5_b_re": nrm(ks[16], s5_shape + (S5_GROUP_CH,), (2 * S5_GROUP_CH) ** -0.5),
        "s5_b_im": nrm(ks[17], s5_shape + (S5_GROUP_CH,), (2 * S5_GROUP_CH) ** -0.5),
        "s5_c_re": nrm(ks[18], (L, 2, S5_GROUPS, S5_GROUP_CH, S5_STATE), 0.5),
        "s5_c_im": nrm(ks[19], (L, 2, S5_GROUPS, S5_GROUP_CH, S5_STATE), 0.5),
        "s5_d": nrm(ks[20], (L, S5_W), 0.5),
        "s5_glu_w": nrm(ks[21], (L, S5_W, S5_W), S5_W ** -0.5),
        "s5_glu_b": nrm(ks[22], (L, S5_W), 0.02),
        "ret_decay": ret_base + nrm(ks[23], (L, 2, RET_HEADS), 0.05),
        "peer_wq": nrm(ks[24], (L, D_MODEL, PEER_HEADS * PEER_QUERY_DIM), D_MODEL ** -0.5),
        "peer_keys": nrm(ks[25], (L, 2, PEER_HEADS, PEER_N_KEYS, PEER_HALF), PEER_HALF ** -0.5),
        "peer_u": nrm(ks[26], (L, PEER_EXPERTS, D_MODEL), D_MODEL ** -0.5),
        "peer_v": nrm(ks[27], (L, PEER_EXPERTS, D_MODEL), 1.0),
    }


def reference(x, c, ctx, c_ctx, ada_w, ada_b, norm1_w, norm2_w, w_in, w_out, q_norm_w, k_norm_w,
              attn_sink, s5_a_re, s5_a_im, s5_log_dt, s5_b_re, s5_b_im, s5_c_re, s5_c_im, s5_d,
              s5_glu_w, s5_glu_b, ret_decay, peer_wq, peer_keys, peer_u, peer_v):
    seq_len = x.shape[1]
    ROWS = seq_len // GRID_W
    att_cos, att_sin = axial_rope_tables(ROWS)
    ret_cos, ret_sin = flat_rope_tables(ROWS * GRID_W, RET_HEAD_DIM)
    for l in range(DEPTH):
        need_ctx = l < DEPTH - 1
        mod = adaln_params(c, ada_w[l], ada_b[l])
        mod_c = adaln_params(c_ctx, ada_w[l], ada_b[l])

        h = modulate(rms_norm(x, norm1_w[l]), mod[:, 0, None], mod[:, 1, None])
        hc = modulate(rms_norm(ctx, norm1_w[l]), mod_c[0], mod_c[1])
        aq, ak, av, su, rq, rk, rv, rg = split_projection(h @ w_in[l])
        caq, cak, cav, csu, crq, crk, crv, crg = split_projection(hc @ w_in[l])
        att, att_c = attention_group(aq, ak, av, caq, cak, cav, q_norm_w[l], k_norm_w[l], attn_sink[l],
                                     att_cos, att_sin, need_ctx)
        ssm, ssm_c = s5_group(su, csu, s5_a_re[l], s5_a_im[l], s5_log_dt[l], s5_b_re[l], s5_b_im[l],
                              s5_c_re[l], s5_c_im[l], s5_d[l], s5_glu_w[l], s5_glu_b[l], need_ctx)
        ret, ret_c = retention_group(rq, rk, rv, rg, crq, crk, crv, crg, ret_decay[l],
                                     ret_cos, ret_sin, need_ctx)
        x = x + mod[:, 2, None] * (jnp.concatenate([att, ssm, ret], axis=-1) @ w_out[l])

        h2 = modulate(rms_norm(x, norm2_w[l]), mod[:, 3, None], mod[:, 4, None])
        x = x + mod[:, 5, None] * peer_ffn(h2, peer_wq[l], peer_keys[l], peer_u[l], peer_v[l])

        if need_ctx:
            ctx = ctx + mod_c[2] * (jnp.concatenate([att_c, ssm_c, ret_c], axis=-1) @ w_out[l])
            hc2 = modulate(rms_norm(ctx, norm2_w[l]), mod_c[3], mod_c[4])
            ctx = ctx + mod_c[5] * peer_ffn(hc2, peer_wq[l], peer_keys[l], peer_u[l], peer_v[l])
    return x
```

```python
import functools
import math

import jax
import jax.numpy as jnp
from jax import lax
from jax.experimental import pallas as pl
from jax.experimental.pallas import tpu as pltpu

F32 = jnp.float32
BF16 = jnp.bfloat16
I32 = jnp.int32
HI = lax.Precision.HIGHEST

D_MODEL = 1024
N_MOD = 6
EPS = 1e-6
GRID_W = 64

N_HEADS = 8
N_KV_HEADS = 2
GQA = N_HEADS // N_KV_HEADS
HEAD_DIM = 64
ATT_BLOCK = 128
ROPE_BASE = 10000.0
ATT_Q_W = N_HEADS * HEAD_DIM
ATT_KV_W = N_KV_HEADS * HEAD_DIM

S5_GROUPS = 16
S5_GROUP_CH = 16
S5_STATE = 64
S5_W = S5_GROUPS * S5_GROUP_CH
S5_CHUNK = 32

RET_HEADS = 4
RET_HEAD_DIM = 64
RET_W = RET_HEADS * RET_HEAD_DIM
RET_CHUNK = 128

MIX_W = ATT_Q_W + S5_W + RET_W
PROJ_W = ATT_Q_W + 2 * ATT_KV_W + S5_W + 4 * RET_W

PEER_HEADS = 8
PEER_N_KEYS = 128
PEER_EXPERTS = PEER_N_KEYS * PEER_N_KEYS
PEER_QUERY_DIM = 128
PEER_HALF = PEER_QUERY_DIM // 2
PEER_TOPK = 16

TOKEN_BLOCK = 256
PEER_EXPERT_TILE = 2048
PEER_SUB_TILE = 256
PEER_PITCH = TOKEN_BLOCK + 8
VMEM_LIMIT = 56 * 1024 * 1024


def _cparams(*sem):
    return pltpu.CompilerParams(dimension_semantics=sem, vmem_limit_bytes=VMEM_LIMIT)


def _adaln_kernel(c_ref, w_ref, b_ref, o_ref):
    c = c_ref[...]
    s = c * jax.nn.sigmoid(c)
    o_ref[...] = jnp.dot(s, w_ref[...], precision=HI, preferred_element_type=F32) + b_ref[...]


def _adaln(cpad, w, b):
    rows, d = cpad.shape
    n = w.shape[1]
    tn = 1536
    return pl.pallas_call(
        _adaln_kernel,
        grid=(n // tn,),
        in_specs=[pl.BlockSpec((rows, d), lambda j: (0, 0)),
                  pl.BlockSpec((d, tn), lambda j: (0, j)),
                  pl.BlockSpec((1, tn), lambda j: (0, j))],
        out_specs=pl.BlockSpec((rows, tn), lambda j: (0, j)),
        out_shape=jax.ShapeDtypeStruct((rows, n), F32),
        compiler_params=_cparams("arbitrary"),
        name="adaln",
    )(cpad, w, b.reshape(1, n))


def _proj_kernel(x_ref, mod_ref, nw_ref, w_ref, z_ref):
    x = x_ref[0]
    ms = jnp.mean(x * x, axis=-1, keepdims=True)
    y = x * lax.rsqrt(ms + EPS) * nw_ref[...]
    m = mod_ref[0, 0]
    h = y * (1.0 + m[1:2]) + m[0:1]
    z_ref[0] = jnp.dot(h.astype(BF16), w_ref[...], preferred_element_type=F32)


def _proj(xa, modp, nw, w_bf, ncb):
    b, n, d = xa.shape
    tm = TOKEN_BLOCK
    pw = w_bf.shape[1]
    return pl.pallas_call(
        _proj_kernel,
        grid=(b, n // tm),
        in_specs=[pl.BlockSpec((1, tm, d), lambda i, j: (i, j, 0)),
                  pl.BlockSpec((1, 1, 8, d), lambda i, j: (i, jnp.minimum(j // ncb, 1), 0, 0)),
                  pl.BlockSpec((1, d), lambda i, j: (0, 0)),
                  pl.BlockSpec((d, pw), lambda i, j: (0, 0))],
        out_specs=pl.BlockSpec((1, tm, pw), lambda i, j: (i, j, 0)),
        out_shape=jax.ShapeDtypeStruct((b, n, pw), F32),
        compiler_params=_cparams("arbitrary", "arbitrary"),
        name="proj",
    )(xa, modp, nw.reshape(1, d), w_bf)


def _rope(x, c, s1, s2):
    w = x.shape[-1]
    return x * c + pltpu.roll(x, w - 32, 1) * s1 + pltpu.roll(x, 32, 1) * s2


def _prep_kernel(z_ref, cq_ref, s1q_ref, s2q_ref, cr_ref, s1r_ref, s2r_ref, qnw_ref, knw_ref, bd_ref,
                 qa_ref, ka_ref, va_ref, su_ref, rq_ref, rk_ref, rv_ref, rg_ref):
    o = 0
    aq = z_ref[0, :, o:o + ATT_Q_W]; o += ATT_Q_W
    ak = z_ref[0, :, o:o + ATT_KV_W]; o += ATT_KV_W
    av = z_ref[0, :, o:o + ATT_KV_W]; o += ATT_KV_W
    su = z_ref[0, :, o:o + S5_W]; o += S5_W
    rq = z_ref[0, :, o:o + RET_W]; o += RET_W
    rk = z_ref[0, :, o:o + RET_W]; o += RET_W
    rv = z_ref[0, :, o:o + RET_W]; o += RET_W
    rg = z_ref[0, :, o:o + RET_W]

    bd = bd_ref[...]
    inv = 1.0 / HEAD_DIM
    msq = jnp.dot(aq * aq, bd, precision=HI, preferred_element_type=F32) * inv
    qn = aq * lax.rsqrt(msq + EPS) * qnw_ref[...]
    msk = jnp.dot(ak * ak, bd[:ATT_KV_W, :ATT_KV_W], precision=HI, preferred_element_type=F32) * inv
    kn = ak * lax.rsqrt(msk + EPS) * knw_ref[...]

    cq, s1q, s2q = cq_ref[...], s1q_ref[...], s2q_ref[...]
    qa_ref[0] = (_rope(qn, cq, s1q, s2q) * (HEAD_DIM ** -0.5)).astype(BF16)
    ka_ref[0] = _rope(kn, cq[:, :ATT_KV_W], s1q[:, :ATT_KV_W], s2q[:, :ATT_KV_W]).astype(BF16)
    va_ref[0] = av.astype(BF16)
    su_ref[0] = su
    cr, s1r, s2r = cr_ref[...], s1r_ref[...], s2r_ref[...]
    rq_ref[0] = (_rope(rq, cr, s1r, s2r) * (RET_HEAD_DIM ** -0.5)).astype(BF16)
    rk_ref[0] = _rope(rk, cr, s1r, s2r).astype(BF16)
    rv_ref[0] = rv.astype(BF16)
    rg_ref[0] = rg


def _prep(z, tabs, qnw, knw, bd):
    b, n, pw = z.shape
    tm = TOKEN_BLOCK
    cq, s1q, s2q, cr, s1r, s2r = tabs

    def tok(w):
        return pl.BlockSpec((1, tm, w), lambda j, i: (i, j, 0))

    def tab(w):
        return pl.BlockSpec((tm, w), lambda j, i: (j, 0))

    def full(shape):
        return pl.BlockSpec(shape, lambda j, i: (0,) * len(shape))

    widths = (ATT_Q_W, ATT_KV_W, ATT_KV_W, S5_W, RET_W, RET_W, RET_W, RET_W)
    dtypes = (BF16, BF16, BF16, F32, BF16, BF16, BF16, F32)
    return pl.pallas_call(
        _prep_kernel,
        grid=(n // tm, b),
        in_specs=[tok(pw), tab(ATT_Q_W), tab(ATT_Q_W), tab(ATT_Q_W), tab(RET_W), tab(RET_W), tab(RET_W),
                  full((1, ATT_Q_W)), full((1, ATT_KV_W)), full((ATT_Q_W, ATT_Q_W))],
        out_specs=[tok(w) for w in widths],
        out_shape=[jax.ShapeDtypeStruct((b, n, w), dt) for w, dt in zip(widths, dtypes)],
        compiler_params=_cparams("arbitrary", "arbitrary"),
        name="prep",
    )(z, cq, s1q, s2q, cr, s1r, s2r, qnw, knw, bd)


def _attn_kernel(sink_ref, q_ref, k_ref, v_ref, o_ref, *, lc, l):
    blk = ATT_BLOCK
    nb = l // blk
    n = pl.program_id(1) - lc // blk
    ctx_off = jnp.where(n >= 0, 0, -(1 << 20))

    def kstart(m):
        return pl.multiple_of(lc + jnp.clip(m, 0, nb - 1) * blk, blk)

    starts = (kstart(n - 1), kstart(n), kstart(n + 1))
    rows = GQA * blk
    row = lax.broadcasted_iota(I32, (rows, 3 * blk), 0)
    col = lax.broadcasted_iota(I32, (rows, 3 * blk), 1)
    qi = row & (blk - 1)
    dist = col - blk - qi
    kpos = (n - 1) * blk + ctx_off + col
    valid = (jnp.abs(dist) <= blk) & (kpos >= 0) & (kpos < l)
    grow = lax.broadcasted_iota(I32, (rows, 1), 0) // blk

    for hkv in range(N_KV_HEADS):
        q4 = jnp.concatenate(
            [q_ref[0, :, (hkv * GQA + g) * HEAD_DIM:(hkv * GQA + g + 1) * HEAD_DIM] for g in range(GQA)], axis=0)
        ksl = slice(hkv * HEAD_DIM, (hkv + 1) * HEAD_DIM)
        kcat = jnp.concatenate([k_ref[0, pl.ds(s, blk), ksl] for s in starts] + [k_ref[0, 0:lc, ksl]], axis=0)
        vcat = jnp.concatenate([v_ref[0, pl.ds(s, blk), ksl] for s in starts] + [v_ref[0, 0:lc, ksl]], axis=0)
        s = lax.dot_general(q4, kcat, (((1,), (1,)), ((), ())), preferred_element_type=F32)
        s_win = jnp.where(valid, s[:, :3 * blk], -jnp.inf)
        s_ctx = s[:, 3 * blk:]
        sink = jnp.zeros((rows, 1), F32)
        for g in range(GQA):
            sink = jnp.where(grow == g, sink_ref[hkv * GQA + g], sink)
        m = jnp.maximum(jnp.maximum(jnp.max(s_win, axis=-1, keepdims=True),
                                    jnp.max(s_ctx, axis=-1, keepdims=True)), sink)
        e_win = jnp.exp(s_win - m)
        e_ctx = jnp.exp(s_ctx - m)
        den = (jnp.sum(e_win, axis=-1, keepdims=True) + jnp.sum(e_ctx, axis=-1, keepdims=True)
               + jnp.exp(sink - m))
        p = jnp.concatenate([e_win, e_ctx], axis=1).astype(BF16)
        o = jnp.dot(p, vcat, preferred_element_type=F32) / den
        for g in range(GQA):
            c0 = (hkv * GQA + g) * HEAD_DIM
            o_ref[0, :, c0:c0 + HEAD_DIM] = o[g * blk:(g + 1) * blk].astype(o_ref.dtype)


def _attention(sink, qa, ka, va, lc):
    b, n, _ = qa.shape
    return pl.pallas_call(
        functools.partial(_attn_kernel, lc=lc, l=n - lc),
        grid=(b, n // ATT_BLOCK),
        in_specs=[pl.BlockSpec(memory_space=pltpu.SMEM),
                  pl.BlockSpec((1, ATT_BLOCK, ATT_Q_W), lambda i, j: (i, j, 0)),
                  pl.BlockSpec((1, n, ATT_KV_W), lambda i, j: (i, 0, 0)),
                  pl.BlockSpec((1, n, ATT_KV_W), lambda i, j: (i, 0, 0))],
        out_specs=pl.BlockSpec((1, ATT_BLOCK, ATT_Q_W), lambda i, j: (i, j, 0)),
        out_shape=jax.ShapeDtypeStruct((b, n, ATT_Q_W), BF16),
        compiler_params=_cparams("arbitrary", "arbitrary"),
        name="attention",
    )(sink, qa, ka, va)


def _s5_kernel(u_ref, m_ref, bf_ref, bb_ref, vf_ref, vb_ref, af_ref, ab_ref, y_ref,
               incf, incb, hinf, hinb, *, nchunk, nchunk_ctx, nb):
    u = u_ref[0]
    incf[...] = jnp.dot(u, bf_ref[0], precision=HI, preferred_element_type=F32)
    incb[...] = jnp.dot(u, bb_ref[0], precision=HI, preferred_element_type=F32)
    af = af_ref[0]
    ab = ab_ref[0]
    zero = jnp.zeros((nb, 2 * S5_STATE), F32)

    def advance(h, a):
        return h * a[0:1] + pltpu.roll(h, S5_STATE, 1) * a[1:2]

    def fwd(c, h):
        r = pl.multiple_of(c * nb, nb)
        hinf[pl.ds(r, nb), :] = h
        return advance(h, af) + incf[pl.ds(r, nb), :]

    lax.fori_loop(0, nchunk, fwd, zero)

    def bwd(c, h):
        r = pl.multiple_of(c * nb, nb)
        hinb[pl.ds(r, nb), :] = h
        return advance(h, ab) + incb[pl.ds(r, nb), :]

    h = lax.fori_loop(0, nchunk_ctx, lambda i, h: bwd(nchunk_ctx - 1 - i, h), zero)
    lax.fori_loop(0, nchunk - nchunk_ctx, lambda i, h: bwd(nchunk - 1 - i, h), h)

    y_ref[0] = (jnp.dot(u, m_ref[0], precision=HI, preferred_element_type=F32)
                + jnp.dot(hinf[...], vf_ref[0], precision=HI, preferred_element_type=F32)
                + jnp.dot(hinb[...], vb_ref[0], precision=HI, preferred_element_type=F32))


def _s5_scan(ug, mats, nchunk, nchunk_ctx, nb):
    g, r, cw = ug.shape
    m, bf, bb, vf, vb, af, ab = mats
    sw = 2 * S5_STATE

    def blk(shape):
        return pl.BlockSpec((1,) + shape, lambda i: (i, 0, 0))

    return pl.pallas_call(
        functools.partial(_s5_kernel, nchunk=nchunk, nchunk_ctx=nchunk_ctx, nb=nb),
        grid=(g,),
        in_specs=[blk((r, cw)), blk((cw, cw)), blk((cw, sw)), blk((cw, sw)), blk((sw, cw)), blk((sw, cw)),
                  blk((2, sw)), blk((2, sw))],
        out_specs=blk((r, cw)),
        out_shape=jax.ShapeDtypeStruct((g, r, cw), F32),
        scratch_shapes=[pltpu.VMEM((r, sw), F32)] * 4,
        compiler_params=_cparams("arbitrary"),
        name="s5_scan",
    )(ug, m, bf, bb, vf, vb, af, ab)


def _s5_out_kernel(y_ref, u_ref, d_ref, w_ref, b_ref, o_ref):
    y = jax.nn.gelu(y_ref[0] + d_ref[...] * u_ref[0])
    gate = jax.nn.sigmoid(jnp.dot(y.astype(BF16), w_ref[...], preferred_element_type=F32) + b_ref[...])
    o_ref[0] = (y * gate).astype(o_ref.dtype)


def _s5_out(y, u, d, w_bf, bvec):
    b, n, w = y.shape
    tm = TOKEN_BLOCK
    tok = pl.BlockSpec((1, tm, w), lambda i, j: (i, j, 0))
    vec = pl.BlockSpec((1, w), lambda i, j: (0, 0))
    return pl.pallas_call(
        _s5_out_kernel,
        grid=(b, n // tm),
        in_specs=[tok, tok, vec, pl.BlockSpec((w, w), lambda i, j: (0, 0)), vec],
        out_specs=tok,
        out_shape=jax.ShapeDtypeStruct((b, n, w), BF16),
        compiler_params=_cparams("arbitrary", "arbitrary"),
        name="s5_out",
    )(y, u, d.reshape(1, w), w_bf, bvec.reshape(1, w))


def _s5_mats(a_re, a_im, log_dt, b_re, b_im, c_re, c_im):
    t = S5_CHUNK
    lam = lax.complex(a_re.astype(F32), a_im.astype(F32))
    ldt = lam * jnp.exp(log_dt.astype(F32))
    a_bar = jnp.exp(ldt)
    b_bar = ((a_bar - 1.0) / lam)[..., None] * lax.complex(b_re.astype(F32), b_im.astype(F32))
    cm = lax.complex(c_re.astype(F32), c_im.astype(F32))
    steps = jnp.arange(t + 1, dtype=F32)
    pw = jnp.exp(ldt[:, None] * steps[None, :, None, None])
    kern = jnp.real(jnp.einsum('dgip,dkgp,dgpj->dkgij', cm, pw, b_bar, precision=HI))

    s_idx = jnp.arange(t)[:, None]
    t_idx = jnp.arange(t)[None, :]
    dist = t_idx - s_idx
    mf = jnp.where((dist >= 0)[:, :, None, None, None], kern[0][jnp.clip(dist, 0, t)], 0.0)
    mb = jnp.where((dist <= 0)[:, :, None, None, None], kern[1][jnp.clip(-dist, 0, t)], 0.0)
    m = jnp.transpose(mf + mb, (2, 0, 4, 1, 3)).reshape(S5_GROUPS, t * S5_GROUP_CH, t * S5_GROUP_CH)

    def split(zc, axis):
        return jnp.concatenate([jnp.real(zc), jnp.imag(zc)], axis=axis)

    bfc = pw[0, t - 1 - jnp.arange(t)][:, :, :, None] * b_bar[0][None]
    bbc = pw[1, jnp.arange(t)][:, :, :, None] * b_bar[1][None]
    bf = split(jnp.transpose(bfc, (1, 0, 3, 2)), 3).reshape(S5_GROUPS, t * S5_GROUP_CH, 2 * S5_STATE)
    bb = split(jnp.transpose(bbc, (1, 0, 3, 2)), 3).reshape(S5_GROUPS, t * S5_GROUP_CH, 2 * S5_STATE)
    vfc = cm[0][:, None] * pw[0, 1 + jnp.arange(t)].transpose(1, 0, 2)[:, :, None, :]
    vbc = cm[1][:, None] * pw[1, t - jnp.arange(t)].transpose(1, 0, 2)[:, :, None, :]

    def readout(zc):
        zt = jnp.transpose(zc, (0, 3, 1, 2))
        return jnp.concatenate([jnp.real(zt), -jnp.imag(zt)], axis=1).reshape(
            S5_GROUPS, 2 * S5_STATE, t * S5_GROUP_CH)

    def decay(ac):
        re, im = jnp.real(ac), jnp.imag(ac)
        return jnp.stack([jnp.concatenate([re, re], -1), jnp.concatenate([-im, im], -1)], axis=1)

    return m, bf, bb, readout(vfc), readout(vbc), decay(pw[0, t]), decay(pw[1, t])


def _s5_group(su, mats, d, glu_w_bf, glu_b, lc):
    b, n, _ = su.shape
    t = S5_CHUNK
    nchunk = n // t
    bp = -(-b // 8) * 8
    up = jnp.pad(su, ((0, bp - b), (0, 0), (0, 0)))
    ug = up.reshape(bp, nchunk, t, S5_GROUPS, S5_GROUP_CH).transpose(3, 1, 0, 2, 4)
    ug = ug.reshape(S5_GROUPS, nchunk * bp, t * S5_GROUP_CH)
    yg = _s5_scan(ug, mats, nchunk, lc // t, bp)
    y = yg.reshape(S5_GROUPS, nchunk, bp, t, S5_GROUP_CH).transpose(2, 1, 3, 0, 4).reshape(bp, n, S5_W)[:b]
    return _s5_out(y, su, d, glu_w_bf, glu_b)


def _ret_kernel(cd_ref, q_ref, k_ref, v_ref, g_ref, dm_ref, qwf_ref, kwf_ref, qwb_ref, kwb_ref, o_ref,
                sf_store, sf, sb, *, nchunk, nchunk_ctx):
    ck = RET_CHUNK
    hd = RET_HEAD_DIM
    tdims = (((0,), (0,)), ((), ()))

    def rows(c):
        return pl.ds(pl.multiple_of(c * ck, ck), ck)

    sf[...] = jnp.zeros_like(sf)
    sb[...] = jnp.zeros_like(sb)

    def fwd(c, carry):
        kc = k_ref[0, rows(c), :].astype(F32) * kwf_ref[...]
        vc = v_ref[0, rows(c), :]
        for h in range(RET_HEADS):
            hs = slice(h * hd, (h + 1) * hd)
            sf_store[c, h] = sf[h]
            kv = lax.dot_general(kc[:, hs].astype(BF16), vc[:, hs], tdims, preferred_element_type=F32)
            sf[h] = cd_ref[0, h] * sf[h] + kv
        return carry

    lax.fori_loop(0, nchunk, fwd, 0)

    def bwd(c, carry):
        qc = q_ref[0, rows(c), :].astype(F32)
        kc_b = k_ref[0, rows(c), :]
        vc = v_ref[0, rows(c), :]
        gc = g_ref[0, rows(c), :]
        qf = (qc * qwf_ref[...]).astype(BF16)
        qb = (qc * qwb_ref[...]).astype(BF16)
        kb = (kc_b.astype(F32) * kwb_ref[...]).astype(BF16)
        q16 = q_ref[0, rows(c), :]
        for h in range(RET_HEADS):
            hs = slice(h * hd, (h + 1) * hd)
            sc = lax.dot_general(q16[:, hs], kc_b[:, hs], (((1,), (1,)), ((), ())), preferred_element_type=F32)
            sc = (sc * dm_ref[h]).astype(BF16)
            o = jnp.dot(sc, vc[:, hs], preferred_element_type=F32)
            o = o + jnp.dot(qf[:, hs], sf_store[c, h].astype(BF16), preferred_element_type=F32)
            o = o + jnp.dot(qb[:, hs], sb[h].astype(BF16), preferred_element_type=F32)
            o = o * lax.rsqrt(jnp.mean(o * o, axis=-1, keepdims=True) + EPS)
            gh = gc[:, hs]
            o_ref[0, rows(c), hs] = (o * (gh * jax.nn.sigmoid(gh))).astype(o_ref.dtype)
            kv = lax.dot_general(kb[:, hs], vc[:, hs], tdims, preferred_element_type=F32)
            sb[h] = cd_ref[1, h] * sb[h] + kv
        return carry

    lax.fori_loop(0, nchunk_ctx, lambda i, cr: bwd(nchunk_ctx - 1 - i, cr), 0)
    lax.fori_loop(0, nchunk - nchunk_ctx, lambda i, cr: bwd(nchunk - 1 - i, cr), 0)


def _retention(rq, rk, rv, rg, tabs, lc):
    b, n, w = rq.shape
    cd, dm, qwf, kwf, qwb, kwb = tabs
    nchunk = n // RET_CHUNK
    tok = pl.BlockSpec((1, n, w), lambda i: (i, 0, 0))
    tab = pl.BlockSpec((RET_CHUNK, w), lambda i: (0, 0))
    st = pltpu.VMEM((RET_HEADS, RET_HEAD_DIM, RET_HEAD_DIM), F32)
    return pl.pallas_call(
        functools.partial(_ret_kernel, nchunk=nchunk, nchunk_ctx=lc // RET_CHUNK),
        grid=(b,),
        in_specs=[pl.BlockSpec(memory_space=pltpu.SMEM), tok, tok, tok, tok,
                  pl.BlockSpec((RET_HEADS, RET_CHUNK, RET_CHUNK), lambda i: (0, 0, 0)), tab, tab, tab, tab],
        out_specs=tok,
        out_shape=jax.ShapeDtypeStruct((b, n, w), BF16),
        scratch_shapes=[pltpu.VMEM((nchunk, RET_HEADS, RET_HEAD_DIM, RET_HEAD_DIM), F32), st, st],
        compiler_params=_cparams("arbitrary"),
        name="retention",
    )(cd, rq, rk, rv, rg, dm, qwf, kwf, qwb, kwb)


def _ret_tables(decay_logit):
    lg = jax.nn.log_sigmoid(decay_logit.astype(F32))
    pos = jnp.arange(RET_CHUNK, dtype=F32)
    rel = pos[:, None] - pos[None, :]
    dm = jnp.where(rel > 0, jnp.exp(lg[0][:, None, None] * jnp.maximum(rel, 0.0)),
                   jnp.where(rel < 0, jnp.exp(lg[1][:, None, None] * jnp.maximum(-rel, 0.0)), 2.0))

    def wide(e):
        return jnp.repeat(e.T, RET_HEAD_DIM, axis=1)

    qwf = wide(jnp.exp(lg[0][:, None] * (pos + 1.0)))
    kwf = wide(jnp.exp(lg[0][:, None] * (RET_CHUNK - 1.0 - pos)))
    qwb = wide(jnp.exp(lg[1][:, None] * (RET_CHUNK - pos)))
    kwb = wide(jnp.exp(lg[1][:, None] * pos))
    cd = jnp.exp(lg * RET_CHUNK)
    return cd, dm, qwf, kwf, qwb, kwb


def _mix_kernel(x_ref, att_ref, ssm_ref, ret_ref, w_ref, mod_ref, nw_ref, xo_ref, h2_ref):
    m = mod_ref[0, 0]
    y = jnp.dot(att_ref[0], w_ref[0:ATT_Q_W, :], preferred_element_type=F32)
    y = y + jnp.dot(ssm_ref[0], w_ref[ATT_Q_W:ATT_Q_W + S5_W, :], preferred_element_type=F32)
    y = y + jnp.dot(ret_ref[0], w_ref[ATT_Q_W + S5_W:MIX_W, :], preferred_element_type=F32)
    x = x_ref[0] + m[2:3] * y
    xo_ref[0] = x
    ms = jnp.mean(x * x, axis=-1, keepdims=True)
    h = x * lax.rsqrt(ms + EPS) * nw_ref[...]
    h2_ref[0] = (h * (1.0 + m[4:5]) + m[3:4]).astype(BF16)


def _mix(xa, att, ssm, ret, w_bf, modp, nw, ncb):
    b, n, d = xa.shape
    tm = TOKEN_BLOCK

    def tok(w):
        return pl.BlockSpec((1, tm, w), lambda i, j: (i, j, 0))

    return pl.pallas_call(
        _mix_kernel,
        grid=(b, n // tm),
        in_specs=[tok(d), tok(ATT_Q_W), tok(S5_W), tok(RET_W),
                  pl.BlockSpec((MIX_W, d), lambda i, j: (0, 0)),
                  pl.BlockSpec((1, 1, 8, d), lambda i, j: (i, jnp.minimum(j // ncb, 1), 0, 0)),
                  pl.BlockSpec((1, d), lambda i, j: (0, 0))],
        out_specs=[tok(d), tok(d)],
        out_shape=[jax.ShapeDtypeStruct((b, n, d), F32), jax.ShapeDtypeStruct((b, n, d), BF16)],
        compiler_params=_cparams("arbitrary", "arbitrary"),
        name="mix",
    )(xa, att, ssm, ret, w_bf, modp, nw.reshape(1, d))


def _peer_select_kernel(h_ref, wqt_ref, keys_ref, gate_ref, i1_ref, i2_ref,
                        s_sc, t_sc, i_sc, best_sc, sel_sc, g_all, e1_all, e2_all):
    tb = h_ref.shape[0]
    kk = PEER_TOPK
    nk = PEER_N_KEYS
    ninf = -jnp.inf
    qt = lax.dot_general(wqt_ref[...], h_ref[...], (((1,), (1,)), ((), ())), preferred_element_type=F32)
    qt = qt.astype(BF16)
    for hh in range(2 * PEER_HEADS):
        h, half = divmod(hh, 2)
        r0 = h * PEER_QUERY_DIM + half * PEER_HALF
        s_sc[hh] = jnp.dot(keys_ref[half * PEER_HEADS + h], qt[r0:r0 + PEER_HALF],
                           preferred_element_type=F32)

    kiota = lax.broadcasted_iota(I32, (nk, tb), 0)

    def stage1(hh, carry):
        s = s_sc[hh]
        for it in range(kk):
            m = jnp.max(s, axis=0, keepdims=True)
            idx = jnp.min(jnp.where(s == m, kiota, nk), axis=0, keepdims=True)
            t_sc[hh, it:it + 1, :] = m
            i_sc[hh, it:it + 1, :] = idx
            s = jnp.where(kiota == idx, ninf, s)
        return carry

    lax.fori_loop(0, 2 * PEER_HEADS, stage1, 0)

    ncand = 80
    rho = lax.broadcasted_iota(I32, (ncand, tb), 0)
    tile = rho >> 3
    wi = rho & 7
    ca = jnp.where(tile <= 1, 0, jnp.where(tile == 9, 8 + wi, tile - 1))
    cb = jnp.where(tile == 1, 8 + wi, jnp.where(tile == 9, 0, wi))
    flat = ca * kk + cb
    cvalid = (ca + 1) * (cb + 1) <= kk

    def stage2(h, carry):
        t1 = t_sc[2 * h]
        t2 = t_sc[2 * h + 1]
        tiles = [t1[0:1] + t2[0:8], t1[0:1] + t2[8:16]]
        tiles += [t1[a:a + 1] + t2[0:8] for a in range(1, 8)]
        tiles += [t1[8:16] + t2[0:1]]
        cand = jnp.where(cvalid, jnp.concatenate(tiles, axis=0), ninf)
        for it in range(kk):
            m = jnp.max(cand, axis=0, keepdims=True)
            idx = jnp.min(jnp.where(cand == m, flat, kk * kk), axis=0, keepdims=True)
            best_sc[it:it + 1, :] = m
            sel_sc[it:it + 1, :] = idx
            cand = jnp.where(flat == idx, ninf, cand)
        best = best_sc[...]
        sel = sel_sc[...]
        asel = sel >> 4
        bsel = sel & (kk - 1)
        i1 = i_sc[2 * h]
        i2 = i_sc[2 * h + 1]
        e1 = jnp.zeros((kk, tb), I32)
        e2 = jnp.zeros((kk, tb), I32)
        for a in range(kk):
            e1 = jnp.where(asel == a, i1[a:a + 1], e1)
            e2 = jnp.where(bsel == a, i2[a:a + 1], e2)
        ex = jnp.exp(best - best[0:1])
        gate = ex / jnp.sum(ex, axis=0, keepdims=True)
        r = pl.ds(pl.multiple_of(h * kk, kk), kk)
        g_all[r, :] = gate
        e1_all[r, :] = e1
        e2_all[r, :] = e2
        return carry

    lax.fori_loop(0, PEER_HEADS, stage2, 0)
    gate_ref[...] = g_all[...].T
    i1_ref[...] = e1_all[...].T
    i2_ref[...] = e2_all[...].T


def _peer_select(h2f, wqt_bf, keys_bf):
    nt, d = h2f.shape
    tb = TOKEN_BLOCK
    r = PEER_HEADS * PEER_TOPK
    row = pl.BlockSpec((tb, r), lambda i: (i, 0))
    return pl.pallas_call(
        _peer_select_kernel,
        grid=(nt // tb,),
        in_specs=[pl.BlockSpec((tb, d), lambda i: (i, 0)),
                  pl.BlockSpec((d, d), lambda i: (0, 0)),
                  pl.BlockSpec((2 * PEER_HEADS, PEER_N_KEYS, PEER_HALF), lambda i: (0, 0, 0))],
        out_specs=[row, row, row],
        out_shape=[jax.ShapeDtypeStruct((nt, r), F32), jax.ShapeDtypeStruct((nt, r), I32),
                   jax.ShapeDtypeStruct((nt, r), I32)],
        scratch_shapes=[pltpu.VMEM((2 * PEER_HEADS, PEER_N_KEYS, tb), F32),
                        pltpu.VMEM((2 * PEER_HEADS, PEER_TOPK, tb), F32),
                        pltpu.VMEM((2 * PEER_HEADS, PEER_TOPK, tb), I32),
                        pltpu.VMEM((PEER_TOPK, tb), F32),
                        pltpu.VMEM((PEER_TOPK, tb), I32),
                        pltpu.VMEM((r, tb), F32),
                        pltpu.VMEM((r, tb), I32),
                        pltpu.VMEM((r, tb), I32)],
        compiler_params=_cparams("arbitrary"),
        name="peer_select",
    )(h2f, wqt_bf, keys_bf)


def _peer_kernel(gate_ref, i1_ref, i2_ref, h_ref, u_ref, v_ref, x_ref, mod_ref, o_ref, gw, acc):
    tb = h_ref.shape[0]
    nk = PEER_N_KEYS
    j = pl.program_id(1)

    @pl.when(j == 0)
    def _():
        siota = lax.broadcasted_iota(I32, (nk, nk), 0)

        def build(t, carry):
            row = pl.ds(t, 1)
            pt = jnp.where(siota == i1_ref[row, :], gate_ref[row, :], 0.0).astype(BF16)
            qt = jnp.where(siota == i2_ref[row, :], 1.0, 0.0).astype(BF16)
            g = lax.dot_general(pt, qt, (((1,), (1,)), ((), ())), preferred_element_type=F32)
            gw[pl.ds(t, nk, stride=PEER_PITCH), :] = g
            return carry

        lax.fori_loop(0, tb, build, 0)
        acc[...] = jnp.zeros_like(acc)

    h = h_ref[...]
    sub = PEER_SUB_TILE
    per = sub // nk
    for s in range(PEER_EXPERT_TILE // sub):
        a = lax.dot_general(h, u_ref[s * sub:(s + 1) * sub, :], (((1,), (1,)), ((), ())),
                            preferred_element_type=F32)
        k1 = (j * (PEER_EXPERT_TILE // nk) + s * per)
        gs = jnp.concatenate(
            [gw[pl.ds(pl.multiple_of((k1 + p) * PEER_PITCH, 8), tb), :] for p in range(per)], axis=1)
        w = (gs * jax.nn.gelu(a)).astype(BF16)
        acc[...] += jnp.dot(w, v_ref[s * sub:(s + 1) * sub, :], preferred_element_type=F32)

    @pl.when(j == pl.num_programs(1) - 1)
    def _():
        o_ref[...] = x_ref[...] + mod_ref[0, 0][5:6] * acc[...]


def _peer(gate, i1, i2, h2f, u_bf, v_bf, xf, modp, blocks_per_batch, ncb):
    nt, d = h2f.shape
    tb = TOKEN_BLOCK
    r = PEER_HEADS * PEER_TOPK
    te = PEER_EXPERT_TILE
    row = pl.BlockSpec((tb, r), lambda i, j: (i, 0))
    tok = pl.BlockSpec((tb, d), lambda i, j: (i, 0))
    return pl.pallas_call(
        _peer_kernel,
        grid=(nt // tb, PEER_EXPERTS // te),
        in_specs=[row, row, row, tok,
                  pl.BlockSpec((te, d), lambda i, j: (j, 0)),
                  pl.BlockSpec((te, d), lambda i, j: (j, 0)),
                  tok,
                  pl.BlockSpec((1, 1, 8, d), lambda i, j: (i // blocks_per_batch,
                                                           jnp.minimum((i % blocks_per_batch) // ncb, 1), 0, 0))],
        out_specs=tok,
        out_shape=jax.ShapeDtypeStruct((nt, d), F32),
        scratch_shapes=[pltpu.VMEM((PEER_N_KEYS * PEER_PITCH, PEER_N_KEYS), F32),
                        pltpu.VMEM((tb, d), F32)],
        compiler_params=_cparams("arbitrary", "arbitrary"),
        name="peer",
    )(gate, i1, i2, h2f, u_bf, v_bf, xf, modp)


def _rope_tables(lc, l):
    t = jnp.arange(l)
    r = (t // GRID_W).astype(F32)
    col = (t % GRID_W).astype(F32)
    pairs = HEAD_DIM // 4
    inv = ROPE_BASE ** (-jnp.arange(pairs, dtype=F32) / pairs)
    ang_a = jnp.concatenate([r[:, None] * inv, col[:, None] * inv], axis=-1)
    half = RET_HEAD_DIM // 2
    inv_r = ROPE_BASE ** (-jnp.arange(half, dtype=F32) / half)
    ang_r = jnp.arange(l, dtype=F32)[:, None] * inv_r

    def tables(ang, heads):
        cos, sin = jnp.cos(ang), jnp.sin(ang)
        zero = jnp.zeros_like(sin)
        c = jnp.concatenate([cos, cos], -1)
        s1 = jnp.concatenate([-sin, zero], -1)
        s2 = jnp.concatenate([zero, sin], -1)
        w = c.shape[-1]
        ident = (jnp.ones((lc, w), F32), jnp.zeros((lc, w), F32), jnp.zeros((lc, w), F32))
        return tuple(jnp.tile(jnp.concatenate([i, x], 0), (1, heads)) for i, x in zip(ident, (c, s1, s2)))

    return tables(ang_a, N_HEADS) + tables(ang_r, RET_HEADS)


def kernel(x, c, ctx, c_ctx, ada_w, ada_b, norm1_w, norm2_w, w_in, w_out, q_norm_w, k_norm_w, attn_sink,
           s5_a_re, s5_a_im, s5_log_dt, s5_b_re, s5_b_im, s5_c_re, s5_c_im, s5_d, s5_glu_w, s5_glu_b,
           ret_decay, peer_wq, peer_keys, peer_u, peer_v):
    b, l, d = x.shape
    lc = ctx.shape[1]
    n = lc + l
    depth = ada_w.shape[0]
    tm = TOKEN_BLOCK
    ncb = lc // tm
    bpb = n // tm

    xa = jnp.concatenate([ctx, x], axis=1)
    rope_tabs = _rope_tables(lc, l)
    head_of = jnp.arange(ATT_Q_W) // HEAD_DIM
    bd = (head_of[:, None] == head_of[None, :]).astype(F32)
    crows = -(-(b + 1) // 8) * 8
    cpad = jnp.concatenate([c, c_ctx[None], jnp.zeros((crows - b - 1, d), F32)], axis=0)

    for layer in range(depth):
        mods = _adaln(cpad, ada_w[layer], ada_b[layer])
        mod = mods[:b].reshape(b, N_MOD, d)
        mod_c = jnp.broadcast_to(mods[b].reshape(1, N_MOD, d), (b, N_MOD, d))
        modp = jnp.pad(jnp.stack([mod_c, mod], axis=1), ((0, 0), (0, 0), (0, 8 - N_MOD), (0, 0)))

        z = _proj(xa, modp, norm1_w[layer], w_in[layer].astype(BF16), ncb)
        qnw = jnp.tile(q_norm_w[layer], N_HEADS).reshape(1, ATT_Q_W)
        knw = jnp.tile(k_norm_w[layer], N_KV_HEADS).reshape(1, ATT_KV_W)
        qa, ka, va, su, rq, rk, rv, rg = _prep(z, rope_tabs, qnw, knw, bd)

        att = _attention(attn_sink[layer].astype(F32), qa, ka, va, lc)
        mats = _s5_mats(s5_a_re[layer], s5_a_im[layer], s5_log_dt[layer], s5_b_re[layer], s5_b_im[layer],
                        s5_c_re[layer], s5_c_im[layer])
        ssm = _s5_group(su, mats, s5_d[layer], s5_glu_w[layer].astype(BF16), s5_glu_b[layer], lc)
        ret = _retention(rq, rk, rv, rg, _ret_tables(ret_decay[layer]), lc)

        xm, h2 = _mix(xa, att, ssm, ret, w_out[layer].astype(BF16), modp, norm2_w[layer], ncb)

        h2f = h2.reshape(b * n, d)
        keys = peer_keys[layer].reshape(2 * PEER_HEADS, PEER_N_KEYS, PEER_HALF).astype(BF16)
        gate, i1, i2 = _peer_select(h2f, peer_wq[layer].T.astype(BF16), keys)
        xf = _peer(gate, i1, i2, h2f, peer_u[layer].astype(BF16), peer_v[layer].astype(BF16),
                   xm.reshape(b * n, d), modp, bpb, ncb)
        xa = xf.reshape(b, n, d)

    return xa[:, lc:]
```

```python
import functools
import math

import jax
import jax.numpy as jnp
from jax import lax
from jax.experimental import pallas as pl
from jax.experimental.pallas import tpu as pltpu

F32 = jnp.float32
BF16 = jnp.bfloat16
I32 = jnp.int32
HI = lax.Precision.HIGHEST

D_MODEL = 1024
N_MOD = 6
EPS = 1e-6
GRID_W = 64

N_HEADS = 8
N_KV_HEADS = 2
GQA = N_HEADS // N_KV_HEADS
HEAD_DIM = 64
ATT_BLOCK = 128
ROPE_BASE = 10000.0
ATT_Q_W = N_HEADS * HEAD_DIM
ATT_KV_W = N_KV_HEADS * HEAD_DIM

S5_GROUPS = 16
S5_GROUP_CH = 16
S5_STATE = 64
S5_W = S5_GROUPS * S5_GROUP_CH
S5_CHUNK = 32

RET_HEADS = 4
RET_HEAD_DIM = 64
RET_W = RET_HEADS * RET_HEAD_DIM
RET_CHUNK = 128

MIX_W = ATT_Q_W + S5_W + RET_W
PROJ_W = ATT_Q_W + 2 * ATT_KV_W + S5_W + 4 * RET_W

PEER_HEADS = 8
PEER_N_KEYS = 128
PEER_EXPERTS = PEER_N_KEYS * PEER_N_KEYS
PEER_QUERY_DIM = 128
PEER_HALF = PEER_QUERY_DIM // 2
PEER_TOPK = 16

TOKEN_BLOCK = 256
PEER_EXPERT_TILE = 2048
PEER_SUB_TILE = 256
PEER_PITCH = PEER_N_KEYS + 8
VMEM_LIMIT = 56 * 1024 * 1024


def _cparams(*sem):
    return pltpu.CompilerParams(dimension_semantics=sem, vmem_limit_bytes=VMEM_LIMIT)


def _adaln_kernel(c_ref, w_ref, b_ref, o_ref):
    c = c_ref[...]
    s = c * jax.nn.sigmoid(c)
    o_ref[...] = jnp.dot(s, w_ref[...], precision=HI, preferred_element_type=F32) + b_ref[...]


def _adaln(cpad, w, b):
    rows, d = cpad.shape
    n = w.shape[1]
    tn = 1536
    return pl.pallas_call(
        _adaln_kernel,
        grid=(n // tn,),
        in_specs=[pl.BlockSpec((rows, d), lambda j: (0, 0)),
                  pl.BlockSpec((d, tn), lambda j: (0, j)),
                  pl.BlockSpec((1, tn), lambda j: (0, j))],
        out_specs=pl.BlockSpec((rows, tn), lambda j: (0, j)),
        out_shape=jax.ShapeDtypeStruct((rows, n), F32),
        compiler_params=_cparams("arbitrary"),
        name="adaln",
    )(cpad, w, b.reshape(1, n))


def _proj_kernel(x_ref, mod_ref, nw_ref, w_ref, z_ref):
    x = x_ref[0]
    ms = jnp.mean(x * x, axis=-1, keepdims=True)
    y = x * lax.rsqrt(ms + EPS) * nw_ref[...]
    m = mod_ref[0, 0]
    h = y * (1.0 + m[1:2]) + m[0:1]
    z_ref[0] = jnp.dot(h.astype(BF16), w_ref[...], preferred_element_type=F32)


def _proj(xa, modp, nw, w_bf, ncb):
    b, n, d = xa.shape
    tm = TOKEN_BLOCK
    pw = w_bf.shape[1]
    return pl.pallas_call(
        _proj_kernel,
        grid=(b, n // tm),
        in_specs=[pl.BlockSpec((1, tm, d), lambda i, j: (i, j, 0)),
                  pl.BlockSpec((1, 1, 8, d), lambda i, j: (i, jnp.minimum(j // ncb, 1), 0, 0)),
                  pl.BlockSpec((1, d), lambda i, j: (0, 0)),
                  pl.BlockSpec((d, pw), lambda i, j: (0, 0))],
        out_specs=pl.BlockSpec((1, tm, pw), lambda i, j: (i, j, 0)),
        out_shape=jax.ShapeDtypeStruct((b, n, pw), F32),
        compiler_params=_cparams("arbitrary", "arbitrary"),
        name="proj",
    )(xa, modp, nw.reshape(1, d), w_bf)


def _rope(x, c, s1, s2):
    w = x.shape[-1]
    return x * c + pltpu.roll(x, w - 32, 1) * s1 + pltpu.roll(x, 32, 1) * s2


def _prep_kernel(z_ref, cq_ref, s1q_ref, s2q_ref, cr_ref, s1r_ref, s2r_ref, qnw_ref, knw_ref, bd_ref,
                 qa_ref, ka_ref, va_ref, su_ref, rq_ref, rk_ref, rv_ref, rg_ref):
    o = 0
    aq = z_ref[0, :, o:o + ATT_Q_W]; o += ATT_Q_W
    ak = z_ref[0, :, o:o + ATT_KV_W]; o += ATT_KV_W
    av = z_ref[0, :, o:o + ATT_KV_W]; o += ATT_KV_W
    su = z_ref[0, :, o:o + S5_W]; o += S5_W
    rq = z_ref[0, :, o:o + RET_W]; o += RET_W
    rk = z_ref[0, :, o:o + RET_W]; o += RET_W
    rv = z_ref[0, :, o:o + RET_W]; o += RET_W
    rg = z_ref[0, :, o:o + RET_W]

    bd = bd_ref[...]
    inv = 1.0 / HEAD_DIM
    msq = jnp.dot(aq * aq, bd, precision=HI, preferred_element_type=F32) * inv
    qn = aq * lax.rsqrt(msq + EPS) * qnw_ref[...]
    msk = jnp.dot(ak * ak, bd[:ATT_KV_W, :ATT_KV_W], precision=HI, preferred_element_type=F32) * inv
    kn = ak * lax.rsqrt(msk + EPS) * knw_ref[...]

    cq, s1q, s2q = cq_ref[...], s1q_ref[...], s2q_ref[...]
    qa_ref[0] = (_rope(qn, cq, s1q, s2q) * (HEAD_DIM ** -0.5)).astype(BF16)
    ka_ref[0] = _rope(kn, cq[:, :ATT_KV_W], s1q[:, :ATT_KV_W], s2q[:, :ATT_KV_W]).astype(BF16)
    va_ref[0] = av.astype(BF16)
    su_ref[0] = su
    cr, s1r, s2r = cr_ref[...], s1r_ref[...], s2r_ref[...]
    rq_ref[0] = (_rope(rq, cr, s1r, s2r) * (RET_HEAD_DIM ** -0.5)).astype(BF16)
    rk_ref[0] = _rope(rk, cr, s1r, s2r).astype(BF16)
    rv_ref[0] = rv.astype(BF16)
    rg_ref[0] = rg


def _prep(z, tabs, qnw, knw, bd):
    b, n, pw = z.shape
    tm = TOKEN_BLOCK
    cq, s1q, s2q, cr, s1r, s2r = tabs

    def tok(w):
        return pl.BlockSpec((1, tm, w), lambda j, i: (i, j, 0))

    def tab(w):
        return pl.BlockSpec((tm, w), lambda j, i: (j, 0))

    def full(shape):
        return pl.BlockSpec(shape, lambda j, i: (0,) * len(shape))

    widths = (ATT_Q_W, ATT_KV_W, ATT_KV_W, S5_W, RET_W, RET_W, RET_W, RET_W)
    dtypes = (BF16, BF16, BF16, F32, BF16, BF16, BF16, F32)
    return pl.pallas_call(
        _prep_kernel,
        grid=(n // tm, b),
        in_specs=[tok(pw), tab(ATT_Q_W), tab(ATT_Q_W), tab(ATT_Q_W), tab(RET_W), tab(RET_W), tab(RET_W),
                  full((1, ATT_Q_W)), full((1, ATT_KV_W)), full((ATT_Q_W, ATT_Q_W))],
        out_specs=[tok(w) for w in widths],
        out_shape=[jax.ShapeDtypeStruct((b, n, w), dt) for w, dt in zip(widths, dtypes)],
        compiler_params=_cparams("arbitrary", "arbitrary"),
        name="prep",
    )(z, cq, s1q, s2q, cr, s1r, s2r, qnw, knw, bd)


def _attn_kernel(sink_ref, q_ref, k_ref, v_ref, o_ref, *, lc, l):
    blk = ATT_BLOCK
    nb = l // blk
    n = pl.program_id(1) - lc // blk
    ctx_off = jnp.where(n >= 0, 0, -(1 << 20))

    def kstart(m):
        return pl.multiple_of(lc + jnp.clip(m, 0, nb - 1) * blk, blk)

    starts = (kstart(n - 1), kstart(n), kstart(n + 1))
    rows = GQA * blk
    row = lax.broadcasted_iota(I32, (rows, 3 * blk), 0)
    col = lax.broadcasted_iota(I32, (rows, 3 * blk), 1)
    qi = row & (blk - 1)
    dist = col - blk - qi
    kpos = (n - 1) * blk + ctx_off + col
    valid = (jnp.abs(dist) <= blk) & (kpos >= 0) & (kpos < l)
    grow = lax.broadcasted_iota(I32, (rows, 1), 0) // blk

    for hkv in range(N_KV_HEADS):
        q4 = jnp.concatenate(
            [q_ref[0, :, (hkv * GQA + g) * HEAD_DIM:(hkv * GQA + g + 1) * HEAD_DIM] for g in range(GQA)], axis=0)
        ksl = slice(hkv * HEAD_DIM, (hkv + 1) * HEAD_DIM)
        kcat = jnp.concatenate([k_ref[0, pl.ds(s, blk), ksl] for s in starts] + [k_ref[0, 0:lc, ksl]], axis=0)
        vcat = jnp.concatenate([v_ref[0, pl.ds(s, blk), ksl] for s in starts] + [v_ref[0, 0:lc, ksl]], axis=0)
        s = lax.dot_general(q4, kcat, (((1,), (1,)), ((), ())), preferred_element_type=F32)
        s_win = jnp.where(valid, s[:, :3 * blk], -jnp.inf)
        s_ctx = s[:, 3 * blk:]
        sink = jnp.zeros((rows, 1), F32)
        for g in range(GQA):
            sink = jnp.where(grow == g, sink_ref[hkv * GQA + g], sink)
        m = jnp.maximum(jnp.maximum(jnp.max(s_win, axis=-1, keepdims=True),
                                    jnp.max(s_ctx, axis=-1, keepdims=True)), sink)
        e_win = jnp.exp(s_win - m)
        e_ctx = jnp.exp(s_ctx - m)
        den = (jnp.sum(e_win, axis=-1, keepdims=True) + jnp.sum(e_ctx, axis=-1, keepdims=True)
               + jnp.exp(sink - m))
        p = jnp.concatenate([e_win, e_ctx], axis=1).astype(BF16)
        o = jnp.dot(p, vcat, preferred_element_type=F32) / den
        for g in range(GQA):
            c0 = (hkv * GQA + g) * HEAD_DIM
            o_ref[0, :, c0:c0 + HEAD_DIM] = o[g * blk:(g + 1) * blk].astype(o_ref.dtype)


def _attention(sink, qa, ka, va, lc):
    b, n, _ = qa.shape
    return pl.pallas_call(
        functools.partial(_attn_kernel, lc=lc, l=n - lc),
        grid=(b, n // ATT_BLOCK),
        in_specs=[pl.BlockSpec(memory_space=pltpu.SMEM),
                  pl.BlockSpec((1, ATT_BLOCK, ATT_Q_W), lambda i, j: (i, j, 0)),
                  pl.BlockSpec((1, n, ATT_KV_W), lambda i, j: (i, 0, 0)),
                  pl.BlockSpec((1, n, ATT_KV_W), lambda i, j: (i, 0, 0))],
        out_specs=pl.BlockSpec((1, ATT_BLOCK, ATT_Q_W), lambda i, j: (i, j, 0)),
        out_shape=jax.ShapeDtypeStruct((b, n, ATT_Q_W), BF16),
        compiler_params=_cparams("arbitrary", "arbitrary"),
        name="attention",
    )(sink, qa, ka, va)


def _s5_kernel(u_ref, m_ref, bf_ref, bb_ref, vf_ref, vb_ref, af_ref, ab_ref, y_ref,
               incf, incb, hinf, hinb, *, nchunk, nchunk_ctx, nb):
    u = u_ref[0]
    incf[...] = jnp.dot(u, bf_ref[0], precision=HI, preferred_element_type=F32)
    incb[...] = jnp.dot(u, bb_ref[0], precision=HI, preferred_element_type=F32)
    af = af_ref[0]
    ab = ab_ref[0]
    zero = jnp.zeros((nb, 2 * S5_STATE), F32)

    def advance(h, a):
        return h * a[0:1] + pltpu.roll(h, S5_STATE, 1) * a[1:2]

    def fwd(c, h):
        r = pl.multiple_of(c * nb, nb)
        hinf[pl.ds(r, nb), :] = h
        return advance(h, af) + incf[pl.ds(r, nb), :]

    lax.fori_loop(0, nchunk, fwd, zero)

    def bwd(c, h):
        r = pl.multiple_of(c * nb, nb)
        hinb[pl.ds(r, nb), :] = h
        return advance(h, ab) + incb[pl.ds(r, nb), :]

    h = lax.fori_loop(0, nchunk_ctx, lambda i, h: bwd(nchunk_ctx - 1 - i, h), zero)
    lax.fori_loop(0, nchunk - nchunk_ctx, lambda i, h: bwd(nchunk - 1 - i, h), h)

    y_ref[0] = (jnp.dot(u, m_ref[0], precision=HI, preferred_element_type=F32)
                + jnp.dot(hinf[...], vf_ref[0], precision=HI, preferred_element_type=F32)
                + jnp.dot(hinb[...], vb_ref[0], precision=HI, preferred_element_type=F32))


def _s5_scan(ug, mats, nchunk, nchunk_ctx, nb):
    g, r, cw = ug.shape
    m, bf, bb, vf, vb, af, ab = mats
    sw = 2 * S5_STATE

    def blk(shape):
        return pl.BlockSpec((1,) + shape, lambda i: (i, 0, 0))

    return pl.pallas_call(
        functools.partial(_s5_kernel, nchunk=nchunk, nchunk_ctx=nchunk_ctx, nb=nb),
        grid=(g,),
        in_specs=[blk((r, cw)), blk((cw, cw)), blk((cw, sw)), blk((cw, sw)), blk((sw, cw)), blk((sw, cw)),
                  blk((2, sw)), blk((2, sw))],
        out_specs=blk((r, cw)),
        out_shape=jax.ShapeDtypeStruct((g, r, cw), F32),
        scratch_shapes=[pltpu.VMEM((r, sw), F32)] * 4,
        compiler_params=_cparams("arbitrary"),
        name="s5_scan",
    )(ug, m, bf, bb, vf, vb, af, ab)


def _s5_out_kernel(y_ref, u_ref, d_ref, w_ref, b_ref, o_ref):
    y = jax.nn.gelu(y_ref[0] + d_ref[...] * u_ref[0])
    gate = jax.nn.sigmoid(jnp.dot(y.astype(BF16), w_ref[...], preferred_element_type=F32) + b_ref[...])
    o_ref[0] = (y * gate).astype(o_ref.dtype)


def _s5_out(y, u, d, w_bf, bvec):
    b, n, w = y.shape
    tm = TOKEN_BLOCK
    tok = pl.BlockSpec((1, tm, w), lambda i, j: (i, j, 0))
    vec = pl.BlockSpec((1, w), lambda i, j: (0, 0))
    return pl.pallas_call(
        _s5_out_kernel,
        grid=(b, n // tm),
        in_specs=[tok, tok, vec, pl.BlockSpec((w, w), lambda i, j: (0, 0)), vec],
        out_specs=tok,
        out_shape=jax.ShapeDtypeStruct((b, n, w), BF16),
        compiler_params=_cparams("arbitrary", "arbitrary"),
        name="s5_out",
    )(y, u, d.reshape(1, w), w_bf, bvec.reshape(1, w))


def _s5_mats(a_re, a_im, log_dt, b_re, b_im, c_re, c_im):
    t = S5_CHUNK
    lam = lax.complex(a_re.astype(F32), a_im.astype(F32))
    ldt = lam * jnp.exp(log_dt.astype(F32))
    a_bar = jnp.exp(ldt)
    b_bar = ((a_bar - 1.0) / lam)[..., None] * lax.complex(b_re.astype(F32), b_im.astype(F32))
    cm = lax.complex(c_re.astype(F32), c_im.astype(F32))
    steps = jnp.arange(t + 1, dtype=F32)
    pw = jnp.exp(ldt[:, None] * steps[None, :, None, None])
    kern = jnp.real(jnp.einsum('dgip,dkgp,dgpj->dkgij', cm, pw, b_bar, precision=HI))

    s_idx = jnp.arange(t)[:, None]
    t_idx = jnp.arange(t)[None, :]
    dist = t_idx - s_idx
    mf = jnp.where((dist >= 0)[:, :, None, None, None], kern[0][jnp.clip(dist, 0, t)], 0.0)
    mb = jnp.where((dist <= 0)[:, :, None, None, None], kern[1][jnp.clip(-dist, 0, t)], 0.0)
    m = jnp.transpose(mf + mb, (2, 0, 4, 1, 3)).reshape(S5_GROUPS, t * S5_GROUP_CH, t * S5_GROUP_CH)

    def split(zc, axis):
        return jnp.concatenate([jnp.real(zc), jnp.imag(zc)], axis=axis)

    bfc = pw[0, t - 1 - jnp.arange(t)][:, :, :, None] * b_bar[0][None]
    bbc = pw[1, jnp.arange(t)][:, :, :, None] * b_bar[1][None]
    bf = split(jnp.transpose(bfc, (1, 0, 3, 2)), 3).reshape(S5_GROUPS, t * S5_GROUP_CH, 2 * S5_STATE)
    bb = split(jnp.transpose(bbc, (1, 0, 3, 2)), 3).reshape(S5_GROUPS, t * S5_GROUP_CH, 2 * S5_STATE)
    vfc = cm[0][:, None] * pw[0, 1 + jnp.arange(t)].transpose(1, 0, 2)[:, :, None, :]
    vbc = cm[1][:, None] * pw[1, t - jnp.arange(t)].transpose(1, 0, 2)[:, :, None, :]

    def readout(zc):
        zt = jnp.transpose(zc, (0, 3, 1, 2))
        return jnp.concatenate([jnp.real(zt), -jnp.imag(zt)], axis=1).reshape(
            S5_GROUPS, 2 * S5_STATE, t * S5_GROUP_CH)

    def decay(ac):
        re, im = jnp.real(ac), jnp.imag(ac)
        return jnp.stack([jnp.concatenate([re, re], -1), jnp.concatenate([-im, im], -1)], axis=1)

    return m, bf, bb, readout(vfc), readout(vbc), decay(pw[0, t]), decay(pw[1, t])


def _s5_group(su, mats, d, glu_w_bf, glu_b, lc):
    b, n, _ = su.shape
    t = S5_CHUNK
    nchunk = n // t
    bp = -(-b // 8) * 8
    up = jnp.pad(su, ((0, bp - b), (0, 0), (0, 0)))
    ug = up.reshape(bp, nchunk, t, S5_GROUPS, S5_GROUP_CH).transpose(3, 1, 0, 2, 4)
    ug = ug.reshape(S5_GROUPS, nchunk * bp, t * S5_GROUP_CH)
    yg = _s5_scan(ug, mats, nchunk, lc // t, bp)
    y = yg.reshape(S5_GROUPS, nchunk, bp, t, S5_GROUP_CH).transpose(2, 1, 3, 0, 4).reshape(bp, n, S5_W)[:b]
    return _s5_out(y, su, d, glu_w_bf, glu_b)


def _ret_kernel(cd_ref, q_ref, k_ref, v_ref, g_ref, dm_ref, qwf_ref, kwf_ref, qwb_ref, kwb_ref, o_ref,
                sf_store, sf, sb, *, nchunk, nchunk_ctx):
    ck = RET_CHUNK
    hd = RET_HEAD_DIM
    tdims = (((0,), (0,)), ((), ()))

    def rows(c):
        return pl.ds(pl.multiple_of(c * ck, ck), ck)

    sf[...] = jnp.zeros_like(sf)
    sb[...] = jnp.zeros_like(sb)

    def fwd(c, carry):
        kc = k_ref[0, rows(c), :].astype(F32) * kwf_ref[...]
        vc = v_ref[0, rows(c), :]
        for h in range(RET_HEADS):
            hs = slice(h * hd, (h + 1) * hd)
            sf_store[c, h] = sf[h]
            kv = lax.dot_general(kc[:, hs].astype(BF16), vc[:, hs], tdims, preferred_element_type=F32)
            sf[h] = cd_ref[0, h] * sf[h] + kv
        return carry

    lax.fori_loop(0, nchunk, fwd, 0)

    def bwd(c, carry):
        qc = q_ref[0, rows(c), :].astype(F32)
        kc_b = k_ref[0, rows(c), :]
        vc = v_ref[0, rows(c), :]
        gc = g_ref[0, rows(c), :]
        qf = (qc * qwf_ref[...]).astype(BF16)
        qb = (qc * qwb_ref[...]).astype(BF16)
        kb = (kc_b.astype(F32) * kwb_ref[...]).astype(BF16)
        q16 = q_ref[0, rows(c), :]
        for h in range(RET_HEADS):
            hs = slice(h * hd, (h + 1) * hd)
            sc = lax.dot_general(q16[:, hs], kc_b[:, hs], (((1,), (1,)), ((), ())), preferred_element_type=F32)
            sc = (sc * dm_ref[h]).astype(BF16)
            o = jnp.dot(sc, vc[:, hs], preferred_element_type=F32)
            o = o + jnp.dot(qf[:, hs], sf_store[c, h].astype(BF16), preferred_element_type=F32)
            o = o + jnp.dot(qb[:, hs], sb[h].astype(BF16), preferred_element_type=F32)
            o = o * lax.rsqrt(jnp.mean(o * o, axis=-1, keepdims=True) + EPS)
            gh = gc[:, hs]
            o_ref[0, rows(c), hs] = (o * (gh * jax.nn.sigmoid(gh))).astype(o_ref.dtype)
            kv = lax.dot_general(kb[:, hs], vc[:, hs], tdims, preferred_element_type=F32)
            sb[h] = cd_ref[1, h] * sb[h] + kv
        return carry

    lax.fori_loop(0, nchunk_ctx, lambda i, cr: bwd(nchunk_ctx - 1 - i, cr), 0)
    lax.fori_loop(0, nchunk - nchunk_ctx, lambda i, cr: bwd(nchunk - 1 - i, cr), 0)


def _retention(rq, rk, rv, rg, tabs, lc):
    b, n, w = rq.shape
    cd, dm, qwf, kwf, qwb, kwb = tabs
    nchunk = n // RET_CHUNK
    tok = pl.BlockSpec((1, n, w), lambda i: (i, 0, 0))
    tab = pl.BlockSpec((RET_CHUNK, w), lambda i: (0, 0))
    st = pltpu.VMEM((RET_HEADS, RET_HEAD_DIM, RET_HEAD_DIM), F32)
    return pl.pallas_call(
        functools.partial(_ret_kernel, nchunk=nchunk, nchunk_ctx=lc // RET_CHUNK),
        grid=(b,),
        in_specs=[pl.BlockSpec(memory_space=pltpu.SMEM), tok, tok, tok, tok,
                  pl.BlockSpec((RET_HEADS, RET_CHUNK, RET_CHUNK), lambda i: (0, 0, 0)), tab, tab, tab, tab],
        out_specs=tok,
        out_shape=jax.ShapeDtypeStruct((b, n, w), BF16),
        scratch_shapes=[pltpu.VMEM((nchunk, RET_HEADS, RET_HEAD_DIM, RET_HEAD_DIM), F32), st, st],
        compiler_params=_cparams("arbitrary"),
        name="retention",
    )(cd, rq, rk, rv, rg, dm, qwf, kwf, qwb, kwb)


def _ret_tables(decay_logit):
    lg = jax.nn.log_sigmoid(decay_logit.astype(F32))
    pos = jnp.arange(RET_CHUNK, dtype=F32)
    rel = pos[:, None] - pos[None, :]
    dm = jnp.where(rel > 0, jnp.exp(lg[0][:, None, None] * jnp.maximum(rel, 0.0)),
                   jnp.where(rel < 0, jnp.exp(lg[1][:, None, None] * jnp.maximum(-rel, 0.0)), 2.0))

    def wide(e):
        return jnp.repeat(e.T, RET_HEAD_DIM, axis=1)

    qwf = wide(jnp.exp(lg[0][:, None] * (pos + 1.0)))
    kwf = wide(jnp.exp(lg[0][:, None] * (RET_CHUNK - 1.0 - pos)))
    qwb = wide(jnp.exp(lg[1][:, None] * (RET_CHUNK - pos)))
    kwb = wide(jnp.exp(lg[1][:, None] * pos))
    cd = jnp.exp(lg * RET_CHUNK)
    return cd, dm, qwf, kwf, qwb, kwb


def _mix_kernel(x_ref, att_ref, ssm_ref, ret_ref, w_ref, mod_ref, nw_ref, xo_ref, h2_ref):
    m = mod_ref[0, 0]
    y = jnp.dot(att_ref[0], w_ref[0:ATT_Q_W, :], preferred_element_type=F32)
    y = y + jnp.dot(ssm_ref[0], w_ref[ATT_Q_W:ATT_Q_W + S5_W, :], preferred_element_type=F32)
    y = y + jnp.dot(ret_ref[0], w_ref[ATT_Q_W + S5_W:MIX_W, :], preferred_element_type=F32)
    x = x_ref[0] + m[2:3] * y
    xo_ref[0] = x
    ms = jnp.mean(x * x, axis=-1, keepdims=True)
    h = x * lax.rsqrt(ms + EPS) * nw_ref[...]
    h2_ref[0] = (h * (1.0 + m[4:5]) + m[3:4]).astype(BF16)


def _mix(xa, att, ssm, ret, w_bf, modp, nw, ncb):
    b, n, d = xa.shape
    tm = TOKEN_BLOCK

    def tok(w):
        return pl.BlockSpec((1, tm, w), lambda i, j: (i, j, 0))

    return pl.pallas_call(
        _mix_kernel,
        grid=(b, n // tm),
        in_specs=[tok(d), tok(ATT_Q_W), tok(S5_W), tok(RET_W),
                  pl.BlockSpec((MIX_W, d), lambda i, j: (0, 0)),
                  pl.BlockSpec((1, 1, 8, d), lambda i, j: (i, jnp.minimum(j // ncb, 1), 0, 0)),
                  pl.BlockSpec((1, d), lambda i, j: (0, 0))],
        out_specs=[tok(d), tok(d)],
        out_shape=[jax.ShapeDtypeStruct((b, n, d), F32), jax.ShapeDtypeStruct((b, n, d), BF16)],
        compiler_params=_cparams("arbitrary", "arbitrary"),
        name="mix",
    )(xa, att, ssm, ret, w_bf, modp, nw.reshape(1, d))


def _peer_select_kernel(h_ref, wqt_ref, keys_ref, gate_ref, i1_ref, i2_ref,
                        s_sc, t_sc, i_sc, best_sc, sel_sc, g_all, e1_all, e2_all):
    tb = h_ref.shape[0]
    kk = PEER_TOPK
    nk = PEER_N_KEYS
    ninf = -jnp.inf
    qt = lax.dot_general(wqt_ref[...], h_ref[...], (((1,), (1,)), ((), ())), preferred_element_type=F32)
    qt = qt.astype(BF16)
    for hh in range(2 * PEER_HEADS):
        h, half = divmod(hh, 2)
        r0 = h * PEER_QUERY_DIM + half * PEER_HALF
        s_sc[hh] = jnp.dot(keys_ref[half * PEER_HEADS + h], qt[r0:r0 + PEER_HALF],
                           preferred_element_type=F32)

    kiota = lax.broadcasted_iota(I32, (nk, tb), 0)

    def stage1(hh, carry):
        s = s_sc[hh]
        for it in range(kk):
            m = jnp.max(s, axis=0, keepdims=True)
            idx = jnp.min(jnp.where(s == m, kiota, nk), axis=0, keepdims=True)
            t_sc[hh, it:it + 1, :] = m
            i_sc[hh, it:it + 1, :] = idx
            s = jnp.where(kiota == idx, ninf, s)
        return carry

    lax.fori_loop(0, 2 * PEER_HEADS, stage1, 0)

    ncand = 80
    rho = lax.broadcasted_iota(I32, (ncand, tb), 0)
    tile = rho >> 3
    wi = rho & 7
    ca = jnp.where(tile <= 1, 0, jnp.where(tile == 9, 8 + wi, tile - 1))
    cb = jnp.where(tile == 1, 8 + wi, jnp.where(tile == 9, 0, wi))
    flat = ca * kk + cb
    cvalid = (ca + 1) * (cb + 1) <= kk

    def stage2(h, carry):
        t1 = t_sc[2 * h]
        t2 = t_sc[2 * h + 1]
        tiles = [t1[0:1] + t2[0:8], t1[0:1] + t2[8:16]]
        tiles += [t1[a:a + 1] + t2[0:8] for a in range(1, 8)]
        tiles += [t1[8:16] + t2[0:1]]
        cand = jnp.where(cvalid, jnp.concatenate(tiles, axis=0), ninf)
        for it in range(kk):
            m = jnp.max(cand, axis=0, keepdims=True)
            idx = jnp.min(jnp.where(cand == m, flat, kk * kk), axis=0, keepdims=True)
            best_sc[it:it + 1, :] = m
            sel_sc[it:it + 1, :] = idx
            cand = jnp.where(flat == idx, ninf, cand)
        best = best_sc[...]
        sel = sel_sc[...]
        asel = sel >> 4
        bsel = sel & (kk - 1)
        i1 = i_sc[2 * h]
        i2 = i_sc[2 * h + 1]
        e1 = jnp.zeros((kk, tb), I32)
        e2 = jnp.zeros((kk, tb), I32)
        for a in range(kk):
            e1 = jnp.where(asel == a, i1[a:a + 1], e1)
            e2 = jnp.where(bsel == a, i2[a:a + 1], e2)
        ex = jnp.exp(best - best[0:1])
        gate = ex / jnp.sum(ex, axis=0, keepdims=True)
        r = pl.ds(pl.multiple_of(h * kk, kk), kk)
        g_all[r, :] = gate
        e1_all[r, :] = e1
        e2_all[r, :] = e2
        return carry

    lax.fori_loop(0, PEER_HEADS, stage2, 0)
    gate_ref[...] = g_all[...].T
    i1_ref[...] = e1_all[...].T
    i2_ref[...] = e2_all[...].T


def _peer_select(h2f, wqt_bf, keys_bf):
    nt, d = h2f.shape
    tb = TOKEN_BLOCK
    r = PEER_HEADS * PEER_TOPK
    row = pl.BlockSpec((tb, r), lambda i: (i, 0))
    return pl.pallas_call(
        _peer_select_kernel,
        grid=(nt // tb,),
        in_specs=[pl.BlockSpec((tb, d), lambda i: (i, 0)),
                  pl.BlockSpec((d, d), lambda i: (0, 0)),
                  pl.BlockSpec((2 * PEER_HEADS, PEER_N_KEYS, PEER_HALF), lambda i: (0, 0, 0))],
        out_specs=[row, row, row],
        out_shape=[jax.ShapeDtypeStruct((nt, r), F32), jax.ShapeDtypeStruct((nt, r), I32),
                   jax.ShapeDtypeStruct((nt, r), I32)],
        scratch_shapes=[pltpu.VMEM((2 * PEER_HEADS, PEER_N_KEYS, tb), F32),
                        pltpu.VMEM((2 * PEER_HEADS, PEER_TOPK, tb), F32),
                        pltpu.VMEM((2 * PEER_HEADS, PEER_TOPK, tb), I32),
                        pltpu.VMEM((PEER_TOPK, tb), F32),
                        pltpu.VMEM((PEER_TOPK, tb), I32),
                        pltpu.VMEM((r, tb), F32),
                        pltpu.VMEM((r, tb), I32),
                        pltpu.VMEM((r, tb), I32)],
        compiler_params=_cparams("arbitrary"),
        name="peer_select",
    )(h2f, wqt_bf, keys_bf)


def _peer_kernel(gate_ref, i1_ref, i2_ref, h_ref, u_ref, v_ref, x_ref, mod_ref, o_ref, gw, wsc, acc):
    tb = h_ref.shape[0]
    nk = PEER_N_KEYS
    j = pl.program_id(1)

    @pl.when(j == 0)
    def _():
        siota = lax.broadcasted_iota(I32, (nk, nk), 0)

        def build(tg, carry):
            r0 = pl.multiple_of(tg * 8, 8)
            i1t = i1_ref[pl.ds(r0, 8), :]
            i2t = i2_ref[pl.ds(r0, 8), :]
            gt = gate_ref[pl.ds(r0, 8), :]
            for k in range(8):
                pt = jnp.where(siota == i1t[k:k + 1], gt[k:k + 1], 0.0).astype(BF16)
                qt = jnp.where(siota == i2t[k:k + 1], 1.0, 0.0).astype(BF16)
                g = lax.dot_general(pt, qt, (((1,), (1,)), ((), ())), preferred_element_type=F32)
                gw[pl.ds(pl.multiple_of((r0 + k) * PEER_PITCH, 8), nk), :] = g
            return carry

        lax.fori_loop(0, tb // 8, build, 0)
        acc[...] = jnp.zeros_like(acc)

    h = h_ref[...]
    sub = PEER_SUB_TILE
    per = sub // nk
    for s in range(PEER_EXPERT_TILE // sub):
        a = lax.dot_general(h, u_ref[s * sub:(s + 1) * sub, :], (((1,), (1,)), ((), ())),
                            preferred_element_type=F32)
        k1 = (j * (PEER_EXPERT_TILE // nk) + s * per)
        gs = jnp.concatenate(
            [gw[pl.ds(k1 + p, tb, stride=PEER_PITCH), :] for p in range(per)], axis=1)
        wsc[:, s * sub:(s + 1) * sub] = (gs * jax.nn.gelu(a)).astype(BF16)
    acc[...] += jnp.dot(wsc[...], v_ref[...], preferred_element_type=F32)

    @pl.when(j == pl.num_programs(1) - 1)
    def _():
        o_ref[...] = x_ref[...] + mod_ref[0, 0][5:6] * acc[...]


def _peer(gate, i1, i2, h2f, u_bf, v_bf, xf, modp, blocks_per_batch, ncb):
    nt, d = h2f.shape
    tb = TOKEN_BLOCK
    r = PEER_HEADS * PEER_TOPK
    te = PEER_EXPERT_TILE
    row = pl.BlockSpec((tb, r), lambda i, j: (i, 0))
    tok = pl.BlockSpec((tb, d), lambda i, j: (i, 0))
    return pl.pallas_call(
        _peer_kernel,
        grid=(nt // tb, PEER_EXPERTS // te),
        in_specs=[row, row, row, tok,
                  pl.BlockSpec((te, d), lambda i, j: (j, 0)),
                  pl.BlockSpec((te, d), lambda i, j: (j, 0)),
                  tok,
                  pl.BlockSpec((1, 1, 8, d), lambda i, j: (i // blocks_per_batch,
                                                           jnp.minimum((i % blocks_per_batch) // ncb, 1), 0, 0))],
        out_specs=tok,
        out_shape=jax.ShapeDtypeStruct((nt, d), F32),
        scratch_shapes=[pltpu.VMEM((tb * PEER_PITCH, PEER_N_KEYS), F32),
                        pltpu.VMEM((tb, te), BF16),
                        pltpu.VMEM((tb, d), F32)],
        compiler_params=_cparams("arbitrary", "arbitrary"),
        name="peer",
    )(gate, i1, i2, h2f, u_bf, v_bf, xf, modp)


def _rope_tables(lc, l):
    t = jnp.arange(l)
    r = (t // GRID_W).astype(F32)
    col = (t % GRID_W).astype(F32)
    pairs = HEAD_DIM // 4
    inv = ROPE_BASE ** (-jnp.arange(pairs, dtype=F32) / pairs)
    ang_a = jnp.concatenate([r[:, None] * inv, col[:, None] * inv], axis=-1)
    half = RET_HEAD_DIM // 2
    inv_r = ROPE_BASE ** (-jnp.arange(half, dtype=F32) / half)
    ang_r = jnp.arange(l, dtype=F32)[:, None] * inv_r

    def tables(ang, heads):
        cos, sin = jnp.cos(ang), jnp.sin(ang)
        zero = jnp.zeros_like(sin)
        c = jnp.concatenate([cos, cos], -1)
        s1 = jnp.concatenate([-sin, zero], -1)
        s2 = jnp.concatenate([zero, sin], -1)
        w = c.shape[-1]
        ident = (jnp.ones((lc, w), F32), jnp.zeros((lc, w), F32), jnp.zeros((lc, w), F32))
        return tuple(jnp.tile(jnp.concatenate([i, x], 0), (1, heads)) for i, x in zip(ident, (c, s1, s2)))

    return tables(ang_a, N_HEADS) + tables(ang_r, RET_HEADS)


def kernel(x, c, ctx, c_ctx, ada_w, ada_b, norm1_w, norm2_w, w_in, w_out, q_norm_w, k_norm_w, attn_sink,
           s5_a_re, s5_a_im, s5_log_dt, s5_b_re, s5_b_im, s5_c_re, s5_c_im, s5_d, s5_glu_w, s5_glu_b,
           ret_decay, peer_wq, peer_keys, peer_u, peer_v):
    b, l, d = x.shape
    lc = ctx.shape[1]
    n = lc + l
    depth = ada_w.shape[0]
    tm = TOKEN_BLOCK
    ncb = lc // tm
    bpb = n // tm

    xa = jnp.concatenate([ctx, x], axis=1)
    rope_tabs = _rope_tables(lc, l)
    head_of = jnp.arange(ATT_Q_W) // HEAD_DIM
    bd = (head_of[:, None] == head_of[None, :]).astype(F32)
    crows = -(-(b + 1) // 8) * 8
    cpad = jnp.concatenate([c, c_ctx[None], jnp.zeros((crows - b - 1, d), F32)], axis=0)

    for layer in range(depth):
        mods = _adaln(cpad, ada_w[layer], ada_b[layer])
        mod = mods[:b].reshape(b, N_MOD, d)
        mod_c = jnp.broadcast_to(mods[b].reshape(1, N_MOD, d), (b, N_MOD, d))
        modp = jnp.pad(jnp.stack([mod_c, mod], axis=1), ((0, 0), (0, 0), (0, 8 - N_MOD), (0, 0)))

        z = _proj(xa, modp, norm1_w[layer], w_in[layer].astype(BF16), ncb)
        qnw = jnp.tile(q_norm_w[layer], N_HEADS).reshape(1, ATT_Q_W)
        knw = jnp.tile(k_norm_w[layer], N_KV_HEADS).reshape(1, ATT_KV_W)
        qa, ka, va, su, rq, rk, rv, rg = _prep(z, rope_tabs, qnw, knw, bd)

        att = _attention(attn_sink[layer].astype(F32), qa, ka, va, lc)
        mats = _s5_mats(s5_a_re[layer], s5_a_im[layer], s5_log_dt[layer], s5_b_re[layer], s5_b_im[layer],
                        s5_c_re[layer], s5_c_im[layer])
        ssm = _s5_group(su, mats, s5_d[layer], s5_glu_w[layer].astype(BF16), s5_glu_b[layer], lc)
        ret = _retention(rq, rk, rv, rg, _ret_tables(ret_decay[layer]), lc)

        xm, h2 = _mix(xa, att, ssm, ret, w_out[layer].astype(BF16), modp, norm2_w[layer], ncb)

        h2f = h2.reshape(b * n, d)
        keys = peer_keys[layer].reshape(2 * PEER_HEADS, PEER_N_KEYS, PEER_HALF).astype(BF16)
        gate, i1, i2 = _peer_select(h2f, peer_wq[layer].T.astype(BF16), keys)
        xf = _peer(gate, i1, i2, h2f, peer_u[layer].astype(BF16), peer_v[layer].astype(BF16),
                   xm.reshape(b * n, d), modp, bpb, ncb)
        xa = xf.reshape(b, n, d)

    return xa[:, lc:]
```

```python
import functools
import math

import jax
import jax.numpy as jnp
from jax import lax
from jax.experimental import pallas as pl
from jax.experimental.pallas import tpu as pltpu

F32 = jnp.float32
BF16 = jnp.bfloat16
I32 = jnp.int32
HI = lax.Precision.HIGHEST

D_MODEL = 1024
N_MOD = 6
EPS = 1e-6
GRID_W = 64

N_HEADS = 8
N_KV_HEADS = 2
GQA = N_HEADS // N_KV_HEADS
HEAD_DIM = 64
ATT_BLOCK = 128
ROPE_BASE = 10000.0
ATT_Q_W = N_HEADS * HEAD_DIM
ATT_KV_W = N_KV_HEADS * HEAD_DIM

S5_GROUPS = 16
S5_GROUP_CH = 16
S5_STATE = 64
S5_W = S5_GROUPS * S5_GROUP_CH
S5_CHUNK = 32

RET_HEADS = 4
RET_HEAD_DIM = 64
RET_W = RET_HEADS * RET_HEAD_DIM
RET_CHUNK = 128

MIX_W = ATT_Q_W + S5_W + RET_W
PROJ_W = ATT_Q_W + 2 * ATT_KV_W + S5_W + 4 * RET_W

PEER_HEADS = 8
PEER_N_KEYS = 128
PEER_EXPERTS = PEER_N_KEYS * PEER_N_KEYS
PEER_QUERY_DIM = 128
PEER_HALF = PEER_QUERY_DIM // 2
PEER_TOPK = 16

TOKEN_BLOCK = 256
PEER_TOKEN_BLOCK = 384
PEER_EXPERT_TILE = 1024
PEER_PITCH = PEER_N_KEYS // 2 + 8
VMEM_LIMIT = 56 * 1024 * 1024


def _cparams(*sem):
    return pltpu.CompilerParams(dimension_semantics=sem, vmem_limit_bytes=VMEM_LIMIT)


def _adaln_kernel(c_ref, w_ref, b_ref, o_ref):
    c = c_ref[...]
    s = c * jax.nn.sigmoid(c)
    o_ref[...] = jnp.dot(s, w_ref[...], precision=HI, preferred_element_type=F32) + b_ref[...]


def _adaln(cpad, w, b):
    rows, d = cpad.shape
    n = w.shape[1]
    tn = 1536
    return pl.pallas_call(
        _adaln_kernel,
        grid=(n // tn,),
        in_specs=[pl.BlockSpec((rows, d), lambda j: (0, 0)),
                  pl.BlockSpec((d, tn), lambda j: (0, j)),
                  pl.BlockSpec((1, tn), lambda j: (0, j))],
        out_specs=pl.BlockSpec((rows, tn), lambda j: (0, j)),
        out_shape=jax.ShapeDtypeStruct((rows, n), F32),
        compiler_params=_cparams("arbitrary"),
        name="adaln",
    )(cpad, w, b.reshape(1, n))


def _proj_kernel(x_ref, mod_ref, nw_ref, w_ref, z_ref):
    x = x_ref[0]
    ms = jnp.mean(x * x, axis=-1, keepdims=True)
    y = x * lax.rsqrt(ms + EPS) * nw_ref[...]
    m = mod_ref[0, 0]
    h = y * (1.0 + m[1:2]) + m[0:1]
    z_ref[0] = jnp.dot(h.astype(BF16), w_ref[...], preferred_element_type=F32)


def _proj(xa, modp, nw, w_bf, ncb):
    b, n, d = xa.shape
    tm = TOKEN_BLOCK
    pw = w_bf.shape[1]
    return pl.pallas_call(
        _proj_kernel,
        grid=(b, n // tm),
        in_specs=[pl.BlockSpec((1, tm, d), lambda i, j: (i, j, 0)),
                  pl.BlockSpec((1, 1, 8, d), lambda i, j: (i, jnp.minimum(j // ncb, 1), 0, 0)),
                  pl.BlockSpec((1, d), lambda i, j: (0, 0)),
                  pl.BlockSpec((d, pw), lambda i, j: (0, 0))],
        out_specs=pl.BlockSpec((1, tm, pw), lambda i, j: (i, j, 0)),
        out_shape=jax.ShapeDtypeStruct((b, n, pw), F32),
        compiler_params=_cparams("arbitrary", "arbitrary"),
        name="proj",
    )(xa, modp, nw.reshape(1, d), w_bf)


def _rope(x, c, s1, s2):
    w = x.shape[-1]
    return x * c + pltpu.roll(x, w - 32, 1) * s1 + pltpu.roll(x, 32, 1) * s2


def _prep_kernel(z_ref, cq_ref, s1q_ref, s2q_ref, cr_ref, s1r_ref, s2r_ref, qnw_ref, knw_ref, bd_ref,
                 qa_ref, ka_ref, va_ref, su_ref, rq_ref, rk_ref, rv_ref, rg_ref):
    o = 0
    aq = z_ref[0, :, o:o + ATT_Q_W]; o += ATT_Q_W
    ak = z_ref[0, :, o:o + ATT_KV_W]; o += ATT_KV_W
    av = z_ref[0, :, o:o + ATT_KV_W]; o += ATT_KV_W
    su = z_ref[0, :, o:o + S5_W]; o += S5_W
    rq = z_ref[0, :, o:o + RET_W]; o += RET_W
    rk = z_ref[0, :, o:o + RET_W]; o += RET_W
    rv = z_ref[0, :, o:o + RET_W]; o += RET_W
    rg = z_ref[0, :, o:o + RET_W]

    bd = bd_ref[...]
    inv = 1.0 / HEAD_DIM
    msq = jnp.dot(aq * aq, bd, precision=HI, preferred_element_type=F32) * inv
    qn = aq * lax.rsqrt(msq + EPS) * qnw_ref[...]
    msk = jnp.dot(ak * ak, bd[:ATT_KV_W, :ATT_KV_W], precision=HI, preferred_element_type=F32) * inv
    kn = ak * lax.rsqrt(msk + EPS) * knw_ref[...]

    cq, s1q, s2q = cq_ref[...], s1q_ref[...], s2q_ref[...]
    qa_ref[0] = (_rope(qn, cq, s1q, s2q) * (HEAD_DIM ** -0.5)).astype(BF16)
    ka_ref[0] = _rope(kn, cq[:, :ATT_KV_W], s1q[:, :ATT_KV_W], s2q[:, :ATT_KV_W]).astype(BF16)
    va_ref[0] = av.astype(BF16)
    su_ref[0] = su
    cr, s1r, s2r = cr_ref[...], s1r_ref[...], s2r_ref[...]
    rq_ref[0] = (_rope(rq, cr, s1r, s2r) * (RET_HEAD_DIM ** -0.5)).astype(BF16)
    rk_ref[0] = _rope(rk, cr, s1r, s2r).astype(BF16)
    rv_ref[0] = rv.astype(BF16)
    rg_ref[0] = rg


def _prep(z, tabs, qnw, knw, bd):
    b, n, pw = z.shape
    tm = TOKEN_BLOCK
    cq, s1q, s2q, cr, s1r, s2r = tabs

    def tok(w):
        return pl.BlockSpec((1, tm, w), lambda j, i: (i, j, 0))

    def tab(w):
        return pl.BlockSpec((tm, w), lambda j, i: (j, 0))

    def full(shape):
        return pl.BlockSpec(shape, lambda j, i: (0,) * len(shape))

    widths = (ATT_Q_W, ATT_KV_W, ATT_KV_W, S5_W, RET_W, RET_W, RET_W, RET_W)
    dtypes = (BF16, BF16, BF16, F32, BF16, BF16, BF16, F32)
    return pl.pallas_call(
        _prep_kernel,
        grid=(n // tm, b),
        in_specs=[tok(pw), tab(ATT_Q_W), tab(ATT_Q_W), tab(ATT_Q_W), tab(RET_W), tab(RET_W), tab(RET_W),
                  full((1, ATT_Q_W)), full((1, ATT_KV_W)), full((ATT_Q_W, ATT_Q_W))],
        out_specs=[tok(w) for w in widths],
        out_shape=[jax.ShapeDtypeStruct((b, n, w), dt) for w, dt in zip(widths, dtypes)],
        compiler_params=_cparams("arbitrary", "arbitrary"),
        name="prep",
    )(z, cq, s1q, s2q, cr, s1r, s2r, qnw, knw, bd)


def _attn_kernel(sink_ref, q_ref, k_ref, v_ref, o_ref, *, lc, l):
    blk = ATT_BLOCK
    nb = l // blk
    n = pl.program_id(1) - lc // blk
    ctx_off = jnp.where(n >= 0, 0, -(1 << 20))

    def kstart(m):
        return pl.multiple_of(lc + jnp.clip(m, 0, nb - 1) * blk, blk)

    starts = (kstart(n - 1), kstart(n), kstart(n + 1))
    rows = GQA * blk
    row = lax.broadcasted_iota(I32, (rows, 3 * blk), 0)
    col = lax.broadcasted_iota(I32, (rows, 3 * blk), 1)
    qi = row & (blk - 1)
    dist = col - blk - qi
    kpos = (n - 1) * blk + ctx_off + col
    valid = (jnp.abs(dist) <= blk) & (kpos >= 0) & (kpos < l)
    grow = lax.broadcasted_iota(I32, (rows, 1), 0) // blk

    for hkv in range(N_KV_HEADS):
        q4 = jnp.concatenate(
            [q_ref[0, :, (hkv * GQA + g) * HEAD_DIM:(hkv * GQA + g + 1) * HEAD_DIM] for g in range(GQA)], axis=0)
        ksl = slice(hkv * HEAD_DIM, (hkv + 1) * HEAD_DIM)
        kcat = jnp.concatenate([k_ref[0, pl.ds(s, blk), ksl] for s in starts] + [k_ref[0, 0:lc, ksl]], axis=0)
        vcat = jnp.concatenate([v_ref[0, pl.ds(s, blk), ksl] for s in starts] + [v_ref[0, 0:lc, ksl]], axis=0)
        s = lax.dot_general(q4, kcat, (((1,), (1,)), ((), ())), preferred_element_type=F32)
        s_win = jnp.where(valid, s[:, :3 * blk], -jnp.inf)
        s_ctx = s[:, 3 * blk:]
        sink = jnp.zeros((rows, 1), F32)
        for g in range(GQA):
            sink = jnp.where(grow == g, sink_ref[hkv * GQA + g], sink)
        m = jnp.maximum(jnp.maximum(jnp.max(s_win, axis=-1, keepdims=True),
                                    jnp.max(s_ctx, axis=-1, keepdims=True)), sink)
        e_win = jnp.exp(s_win - m)
        e_ctx = jnp.exp(s_ctx - m)
        den = (jnp.sum(e_win, axis=-1, keepdims=True) + jnp.sum(e_ctx, axis=-1, keepdims=True)
               + jnp.exp(sink - m))
        p = jnp.concatenate([e_win, e_ctx], axis=1).astype(BF16)
        o = jnp.dot(p, vcat, preferred_element_type=F32) / den
        for g in range(GQA):
            c0 = (hkv * GQA + g) * HEAD_DIM
            o_ref[0, :, c0:c0 + HEAD_DIM] = o[g * blk:(g + 1) * blk].astype(o_ref.dtype)


def _attention(sink, qa, ka, va, lc):
    b, n, _ = qa.shape
    return pl.pallas_call(
        functools.partial(_attn_kernel, lc=lc, l=n - lc),
        grid=(b, n // ATT_BLOCK),
        in_specs=[pl.BlockSpec(memory_space=pltpu.SMEM),
                  pl.BlockSpec((1, ATT_BLOCK, ATT_Q_W), lambda i, j: (i, j, 0)),
                  pl.BlockSpec((1, n, ATT_KV_W), lambda i, j: (i, 0, 0)),
                  pl.BlockSpec((1, n, ATT_KV_W), lambda i, j: (i, 0, 0))],
        out_specs=pl.BlockSpec((1, ATT_BLOCK, ATT_Q_W), lambda i, j: (i, j, 0)),
        out_shape=jax.ShapeDtypeStruct((b, n, ATT_Q_W), BF16),
        compiler_params=_cparams("arbitrary", "arbitrary"),
        name="attention",
    )(sink, qa, ka, va)


def _s5_kernel(u_ref, m_ref, bf_ref, bb_ref, vf_ref, vb_ref, af_ref, ab_ref, y_ref,
               incf, incb, hinf, hinb, *, nchunk, nchunk_ctx, nb):
    u = u_ref[0]
    incf[...] = jnp.dot(u, bf_ref[0], precision=HI, preferred_element_type=F32)
    incb[...] = jnp.dot(u, bb_ref[0], precision=HI, preferred_element_type=F32)
    af = af_ref[0]
    ab = ab_ref[0]
    zero = jnp.zeros((nb, 2 * S5_STATE), F32)

    def advance(h, a):
        return h * a[0:1] + pltpu.roll(h, S5_STATE, 1) * a[1:2]

    def fwd(c, h):
        r = pl.multiple_of(c * nb, nb)
        hinf[pl.ds(r, nb), :] = h
        return advance(h, af) + incf[pl.ds(r, nb), :]

    lax.fori_loop(0, nchunk, fwd, zero)

    def bwd(c, h):
        r = pl.multiple_of(c * nb, nb)
        hinb[pl.ds(r, nb), :] = h
        return advance(h, ab) + incb[pl.ds(r, nb), :]

    h = lax.fori_loop(0, nchunk_ctx, lambda i, h: bwd(nchunk_ctx - 1 - i, h), zero)
    lax.fori_loop(0, nchunk - nchunk_ctx, lambda i, h: bwd(nchunk - 1 - i, h), h)

    y_ref[0] = (jnp.dot(u, m_ref[0], precision=HI, preferred_element_type=F32)
                + jnp.dot(hinf[...], vf_ref[0], precision=HI, preferred_element_type=F32)
                + jnp.dot(hinb[...], vb_ref[0], precision=HI, preferred_element_type=F32))


def _s5_scan(ug, mats, nchunk, nchunk_ctx, nb):
    g, r, cw = ug.shape
    m, bf, bb, vf, vb, af, ab = mats
    sw = 2 * S5_STATE

    def blk(shape):
        return pl.BlockSpec((1,) + shape, lambda i: (i, 0, 0))

    return pl.pallas_call(
        functools.partial(_s5_kernel, nchunk=nchunk, nchunk_ctx=nchunk_ctx, nb=nb),
        grid=(g,),
        in_specs=[blk((r, cw)), blk((cw, cw)), blk((cw, sw)), blk((cw, sw)), blk((sw, cw)), blk((sw, cw)),
                  blk((2, sw)), blk((2, sw))],
        out_specs=blk((r, cw)),
        out_shape=jax.ShapeDtypeStruct((g, r, cw), F32),
        scratch_shapes=[pltpu.VMEM((r, sw), F32)] * 4,
        compiler_params=_cparams("arbitrary"),
        name="s5_scan",
    )(ug, m, bf, bb, vf, vb, af, ab)


def _s5_out_kernel(y_ref, u_ref, d_ref, w_ref, b_ref, o_ref):
    y = jax.nn.gelu(y_ref[0] + d_ref[...] * u_ref[0])
    gate = jax.nn.sigmoid(jnp.dot(y.astype(BF16), w_ref[...], preferred_element_type=F32) + b_ref[...])
    o_ref[0] = (y * gate).astype(o_ref.dtype)


def _s5_out(y, u, d, w_bf, bvec):
    b, n, w = y.shape
    tm = TOKEN_BLOCK
    tok = pl.BlockSpec((1, tm, w), lambda i, j: (i, j, 0))
    vec = pl.BlockSpec((1, w), lambda i, j: (0, 0))
    return pl.pallas_call(
        _s5_out_kernel,
        grid=(b, n // tm),
        in_specs=[tok, tok, vec, pl.BlockSpec((w, w), lambda i, j: (0, 0)), vec],
        out_specs=tok,
        out_shape=jax.ShapeDtypeStruct((b, n, w), BF16),
        compiler_params=_cparams("arbitrary", "arbitrary"),
        name="s5_out",
    )(y, u, d.reshape(1, w), w_bf, bvec.reshape(1, w))


def _s5_mats(a_re, a_im, log_dt, b_re, b_im, c_re, c_im):
    t = S5_CHUNK
    lam = lax.complex(a_re.astype(F32), a_im.astype(F32))
    ldt = lam * jnp.exp(log_dt.astype(F32))
    a_bar = jnp.exp(ldt)
    b_bar = ((a_bar - 1.0) / lam)[..., None] * lax.complex(b_re.astype(F32), b_im.astype(F32))
    cm = lax.complex(c_re.astype(F32), c_im.astype(F32))
    steps = jnp.arange(t + 1, dtype=F32)
    pw = jnp.exp(ldt[:, None] * steps[None, :, None, None])
    kern = jnp.real(jnp.einsum('dgip,dkgp,dgpj->dkgij', cm, pw, b_bar, precision=HI))

    s_idx = jnp.arange(t)[:, None]
    t_idx = jnp.arange(t)[None, :]
    dist = t_idx - s_idx
    mf = jnp.where((dist >= 0)[:, :, None, None, None], kern[0][jnp.clip(dist, 0, t)], 0.0)
    mb = jnp.where((dist <= 0)[:, :, None, None, None], kern[1][jnp.clip(-dist, 0, t)], 0.0)
    m = jnp.transpose(mf + mb, (2, 0, 4, 1, 3)).reshape(S5_GROUPS, t * S5_GROUP_CH, t * S5_GROUP_CH)

    def split(zc, axis):
        return jnp.concatenate([jnp.real(zc), jnp.imag(zc)], axis=axis)

    bfc = pw[0, t - 1 - jnp.arange(t)][:, :, :, None] * b_bar[0][None]
    bbc = pw[1, jnp.arange(t)][:, :, :, None] * b_bar[1][None]
    bf = split(jnp.transpose(bfc, (1, 0, 3, 2)), 3).reshape(S5_GROUPS, t * S5_GROUP_CH, 2 * S5_STATE)
    bb = split(jnp.transpose(bbc, (1, 0, 3, 2)), 3).reshape(S5_GROUPS, t * S5_GROUP_CH, 2 * S5_STATE)
    vfc = cm[0][:, None] * pw[0, 1 + jnp.arange(t)].transpose(1, 0, 2)[:, :, None, :]
    vbc = cm[1][:, None] * pw[1, t - jnp.arange(t)].transpose(1, 0, 2)[:, :, None, :]

    def readout(zc):
        zt = jnp.transpose(zc, (0, 3, 1, 2))
        return jnp.concatenate([jnp.real(zt), -jnp.imag(zt)], axis=1).reshape(
            S5_GROUPS, 2 * S5_STATE, t * S5_GROUP_CH)

    def decay(ac):
        re, im = jnp.real(ac), jnp.imag(ac)
        return jnp.stack([jnp.concatenate([re, re], -1), jnp.concatenate([-im, im], -1)], axis=1)

    return m, bf, bb, readout(vfc), readout(vbc), decay(pw[0, t]), decay(pw[1, t])


def _s5_group(su, mats, d, glu_w_bf, glu_b, lc):
    b, n, _ = su.shape
    t = S5_CHUNK
    nchunk = n // t
    bp = -(-b // 8) * 8
    up = jnp.pad(su, ((0, bp - b), (0, 0), (0, 0)))
    ug = up.reshape(bp, nchunk, t, S5_GROUPS, S5_GROUP_CH).transpose(3, 1, 0, 2, 4)
    ug = ug.reshape(S5_GROUPS, nchunk * bp, t * S5_GROUP_CH)
    yg = _s5_scan(ug, mats, nchunk, lc // t, bp)
    y = yg.reshape(S5_GROUPS, nchunk, bp, t, S5_GROUP_CH).transpose(2, 1, 3, 0, 4).reshape(bp, n, S5_W)[:b]
    return _s5_out(y, su, d, glu_w_bf, glu_b)


def _ret_kernel(cd_ref, q_ref, k_ref, v_ref, g_ref, dm_ref, qwf_ref, kwf_ref, qwb_ref, kwb_ref, o_ref,
                sf_store, sf, sb, *, nchunk, nchunk_ctx):
    ck = RET_CHUNK
    hd = RET_HEAD_DIM
    tdims = (((0,), (0,)), ((), ()))

    def rows(c):
        return pl.ds(pl.multiple_of(c * ck, ck), ck)

    sf[...] = jnp.zeros_like(sf)
    sb[...] = jnp.zeros_like(sb)

    def fwd(c, carry):
        kc = k_ref[0, rows(c), :].astype(F32) * kwf_ref[...]
        vc = v_ref[0, rows(c), :]
        for h in range(RET_HEADS):
            hs = slice(h * hd, (h + 1) * hd)
            sf_store[c, h] = sf[h]
            kv = lax.dot_general(kc[:, hs].astype(BF16), vc[:, hs], tdims, preferred_element_type=F32)
            sf[h] = cd_ref[0, h] * sf[h] + kv
        return carry

    lax.fori_loop(0, nchunk, fwd, 0)

    def bwd(c, carry):
        qc = q_ref[0, rows(c), :].astype(F32)
        kc_b = k_ref[0, rows(c), :]
        vc = v_ref[0, rows(c), :]
        gc = g_ref[0, rows(c), :]
        qf = (qc * qwf_ref[...]).astype(BF16)
        qb = (qc * qwb_ref[...]).astype(BF16)
        kb = (kc_b.astype(F32) * kwb_ref[...]).astype(BF16)
        q16 = q_ref[0, rows(c), :]
        for h in range(RET_HEADS):
            hs = slice(h * hd, (h + 1) * hd)
            sc = lax.dot_general(q16[:, hs], kc_b[:, hs], (((1,), (1,)), ((), ())), preferred_element_type=F32)
            sc = (sc * dm_ref[h]).astype(BF16)
            o = jnp.dot(sc, vc[:, hs], preferred_element_type=F32)
            o = o + jnp.dot(qf[:, hs], sf_store[c, h].astype(BF16), preferred_element_type=F32)
            o = o + jnp.dot(qb[:, hs], sb[h].astype(BF16), preferred_element_type=F32)
            o = o * lax.rsqrt(jnp.mean(o * o, axis=-1, keepdims=True) + EPS)
            gh = gc[:, hs]
            o_ref[0, rows(c), hs] = (o * (gh * jax.nn.sigmoid(gh))).astype(o_ref.dtype)
            kv = lax.dot_general(kb[:, hs], vc[:, hs], tdims, preferred_element_type=F32)
            sb[h] = cd_ref[1, h] * sb[h] + kv
        return carry

    lax.fori_loop(0, nchunk_ctx, lambda i, cr: bwd(nchunk_ctx - 1 - i, cr), 0)
    lax.fori_loop(0, nchunk - nchunk_ctx, lambda i, cr: bwd(nchunk - 1 - i, cr), 0)


def _retention(rq, rk, rv, rg, tabs, lc):
    b, n, w = rq.shape
    cd, dm, qwf, kwf, qwb, kwb = tabs
    nchunk = n // RET_CHUNK
    tok = pl.BlockSpec((1, n, w), lambda i: (i, 0, 0))
    tab = pl.BlockSpec((RET_CHUNK, w), lambda i: (0, 0))
    st = pltpu.VMEM((RET_HEADS, RET_HEAD_DIM, RET_HEAD_DIM), F32)
    return pl.pallas_call(
        functools.partial(_ret_kernel, nchunk=nchunk, nchunk_ctx=lc // RET_CHUNK),
        grid=(b,),
        in_specs=[pl.BlockSpec(memory_space=pltpu.SMEM), tok, tok, tok, tok,
                  pl.BlockSpec((RET_HEADS, RET_CHUNK, RET_CHUNK), lambda i: (0, 0, 0)), tab, tab, tab, tab],
        out_specs=tok,
        out_shape=jax.ShapeDtypeStruct((b, n, w), BF16),
        scratch_shapes=[pltpu.VMEM((nchunk, RET_HEADS, RET_HEAD_DIM, RET_HEAD_DIM), F32), st, st],
        compiler_params=_cparams("arbitrary"),
        name="retention",
    )(cd, rq, rk, rv, rg, dm, qwf, kwf, qwb, kwb)


def _ret_tables(decay_logit):
    lg = jax.nn.log_sigmoid(decay_logit.astype(F32))
    pos = jnp.arange(RET_CHUNK, dtype=F32)
    rel = pos[:, None] - pos[None, :]
    dm = jnp.where(rel > 0, jnp.exp(lg[0][:, None, None] * jnp.maximum(rel, 0.0)),
                   jnp.where(rel < 0, jnp.exp(lg[1][:, None, None] * jnp.maximum(-rel, 0.0)), 2.0))

    def wide(e):
        return jnp.repeat(e.T, RET_HEAD_DIM, axis=1)

    qwf = wide(jnp.exp(lg[0][:, None] * (pos + 1.0)))
    kwf = wide(jnp.exp(lg[0][:, None] * (RET_CHUNK - 1.0 - pos)))
    qwb = wide(jnp.exp(lg[1][:, None] * (RET_CHUNK - pos)))
    kwb = wide(jnp.exp(lg[1][:, None] * pos))
    cd = jnp.exp(lg * RET_CHUNK)
    return cd, dm, qwf, kwf, qwb, kwb


def _mix_kernel(x_ref, att_ref, ssm_ref, ret_ref, w_ref, mod_ref, nw_ref, xo_ref, h2_ref):
    m = mod_ref[0, 0]
    y = jnp.dot(att_ref[0], w_ref[0:ATT_Q_W, :], preferred_element_type=F32)
    y = y + jnp.dot(ssm_ref[0], w_ref[ATT_Q_W:ATT_Q_W + S5_W, :], preferred_element_type=F32)
    y = y + jnp.dot(ret_ref[0], w_ref[ATT_Q_W + S5_W:MIX_W, :], preferred_element_type=F32)
    x = x_ref[0] + m[2:3] * y
    xo_ref[0] = x
    ms = jnp.mean(x * x, axis=-1, keepdims=True)
    h = x * lax.rsqrt(ms + EPS) * nw_ref[...]
    h2_ref[0] = (h * (1.0 + m[4:5]) + m[3:4]).astype(BF16)


def _mix(xa, att, ssm, ret, w_bf, modp, nw, ncb):
    b, n, d = xa.shape
    tm = TOKEN_BLOCK

    def tok(w):
        return pl.BlockSpec((1, tm, w), lambda i, j: (i, j, 0))

    return pl.pallas_call(
        _mix_kernel,
        grid=(b, n // tm),
        in_specs=[tok(d), tok(ATT_Q_W), tok(S5_W), tok(RET_W),
                  pl.BlockSpec((MIX_W, d), lambda i, j: (0, 0)),
                  pl.BlockSpec((1, 1, 8, d), lambda i, j: (i, jnp.minimum(j // ncb, 1), 0, 0)),
                  pl.BlockSpec((1, d), lambda i, j: (0, 0))],
        out_specs=[tok(d), tok(d)],
        out_shape=[jax.ShapeDtypeStruct((b, n, d), F32), jax.ShapeDtypeStruct((b, n, d), BF16)],
        compiler_params=_cparams("arbitrary", "arbitrary"),
        name="mix",
    )(xa, att, ssm, ret, w_bf, modp, nw.reshape(1, d))


def _peer_select_kernel(h_ref, wqt_ref, keys_ref, gate_ref, i1_ref, i2_ref,
                        s_sc, t_sc, i_sc, best_sc, sel_sc, g_all, e1_all, e2_all):
    tb = h_ref.shape[0]
    kk = PEER_TOPK
    nk = PEER_N_KEYS
    ninf = -jnp.inf
    qt = lax.dot_general(wqt_ref[...], h_ref[...], (((1,), (1,)), ((), ())), preferred_element_type=F32)
    qt = qt.astype(BF16)
    for hh in range(2 * PEER_HEADS):
        h, half = divmod(hh, 2)
        r0 = h * PEER_QUERY_DIM + half * PEER_HALF
        s_sc[hh] = jnp.dot(keys_ref[half * PEER_HEADS + h], qt[r0:r0 + PEER_HALF],
                           preferred_element_type=F32)

    kiota = lax.broadcasted_iota(I32, (nk, tb), 0)

    def stage1(hh, carry):
        s = s_sc[hh]
        for it in range(kk):
            m = jnp.max(s, axis=0, keepdims=True)
            idx = jnp.min(jnp.where(s == m, kiota, nk), axis=0, keepdims=True)
            t_sc[hh, it:it + 1, :] = m
            i_sc[hh, it:it + 1, :] = idx
            s = jnp.where(kiota == idx, ninf, s)
        return carry

    lax.fori_loop(0, 2 * PEER_HEADS, stage1, 0)

    ncand = 80
    rho = lax.broadcasted_iota(I32, (ncand, tb), 0)
    tile = rho >> 3
    wi = rho & 7
    ca = jnp.where(tile <= 1, 0, jnp.where(tile == 9, 8 + wi, tile - 1))
    cb = jnp.where(tile == 1, 8 + wi, jnp.where(tile == 9, 0, wi))
    flat = ca * kk + cb
    cvalid = (ca + 1) * (cb + 1) <= kk

    def stage2(h, carry):
        t1 = t_sc[2 * h]
        t2 = t_sc[2 * h + 1]
        tiles = [t1[0:1] + t2[0:8], t1[0:1] + t2[8:16]]
        tiles += [t1[a:a + 1] + t2[0:8] for a in range(1, 8)]
        tiles += [t1[8:16] + t2[0:1]]
        cand = jnp.where(cvalid, jnp.concatenate(tiles, axis=0), ninf)
        for it in range(kk):
            m = jnp.max(cand, axis=0, keepdims=True)
            idx = jnp.min(jnp.where(cand == m, flat, kk * kk), axis=0, keepdims=True)
            best_sc[it:it + 1, :] = m
            sel_sc[it:it + 1, :] = idx
            cand = jnp.where(flat == idx, ninf, cand)
        best = best_sc[...]
        sel = sel_sc[...]
        asel = sel >> 4
        bsel = sel & (kk - 1)
        i1 = i_sc[2 * h]
        i2 = i_sc[2 * h + 1]
        e1 = jnp.zeros((kk, tb), I32)
        e2 = jnp.zeros((kk, tb), I32)
        for a in range(kk):
            e1 = jnp.where(asel == a, i1[a:a + 1], e1)
            e2 = jnp.where(bsel == a, i2[a:a + 1], e2)
        ex = jnp.exp(best - best[0:1])
        gate = ex / jnp.sum(ex, axis=0, keepdims=True)
        r = pl.ds(pl.multiple_of(h * kk, kk), kk)
        g_all[r, :] = gate
        e1_all[r, :] = e1
        e2_all[r, :] = e2
        return carry

    lax.fori_loop(0, PEER_HEADS, stage2, 0)
    gate_ref[...] = g_all[...].T
    i1_ref[...] = e1_all[...].T
    i2_ref[...] = e2_all[...].T


def _peer_select(h2f, wqt_bf, keys_bf):
    nt, d = h2f.shape
    tb = TOKEN_BLOCK
    r = PEER_HEADS * PEER_TOPK
    row = pl.BlockSpec((tb, r), lambda i: (i, 0))
    return pl.pallas_call(
        _peer_select_kernel,
        grid=(nt // tb,),
        in_specs=[pl.BlockSpec((tb, d), lambda i: (i, 0)),
                  pl.BlockSpec((d, d), lambda i: (0, 0)),
                  pl.BlockSpec((2 * PEER_HEADS, PEER_N_KEYS, PEER_HALF), lambda i: (0, 0, 0))],
        out_specs=[row, row, row],
        out_shape=[jax.ShapeDtypeStruct((nt, r), F32), jax.ShapeDtypeStruct((nt, r), I32),
                   jax.ShapeDtypeStruct((nt, r), I32)],
        scratch_shapes=[pltpu.VMEM((2 * PEER_HEADS, PEER_N_KEYS, tb), F32),
                        pltpu.VMEM((2 * PEER_HEADS, PEER_TOPK, tb), F32),
                        pltpu.VMEM((2 * PEER_HEADS, PEER_TOPK, tb), I32),
                        pltpu.VMEM((PEER_TOPK, tb), F32),
                        pltpu.VMEM((PEER_TOPK, tb), I32),
                        pltpu.VMEM((r, tb), F32),
                        pltpu.VMEM((r, tb), I32),
                        pltpu.VMEM((r, tb), I32)],
        compiler_params=_cparams("arbitrary"),
        name="peer_select",
    )(h2f, wqt_bf, keys_bf)


def _peer_build(rows, dst, row0, count, base):
    gate_ref, i1_ref, i2_ref = rows
    nk = PEER_N_KEYS
    siota = lax.broadcasted_iota(I32, (nk, nk), 0)
    for g8 in range(count // 8):
        r0 = row0 + g8 * 8
        if not isinstance(r0, int):
            r0 = pl.multiple_of(r0, 8)
        i1t = i1_ref[pl.ds(r0, 8), :]
        i2t = i2_ref[pl.ds(r0, 8), :]
        gt = gate_ref[pl.ds(r0, 8), :]
        for k in range(8):
            pt = jnp.where(siota == i1t[k:k + 1], gt[k:k + 1], 0.0).astype(BF16)
            qt = jnp.where(siota == i2t[k:k + 1], 1.0, 0.0)
            g = jnp.dot(pt, qt.T.astype(BF16), preferred_element_type=F32)
            gb = lax.bitcast_convert_type(g.astype(BF16).astype(F32), jnp.uint32)
            pk = (gb[:nk // 2] >> 16) | gb[nk // 2:]
            dst[pl.ds(pl.multiple_of(base + (r0 + k) * PEER_PITCH, 8), nk // 2), :] = pk


def _peer_kernel(rows0_g, rows0_1, rows0_2, rows_g, rows_1, rows_2, h_ref, u_ref, v_ref, x_ref, mod_ref, o_ref,
                 gw, wsc, acc, *, nblocks, ntiles, blocks_per_batch, lc):
    tb = h_ref.shape[0]
    nk = PEER_N_KEYS
    half = PEER_EXPERT_TILE // 2
    per_step = tb // ntiles
    buf_rows = tb * PEER_PITCH
    g = pl.program_id(0)
    blk = jnp.minimum(g // ntiles, nblocks - 1)
    tile = g % ntiles
    ptile = (g + ntiles - 1) % ntiles

    @pl.when(g == 0)
    def _():
        def first(t8, carry):
            _peer_build((rows0_g, rows0_1, rows0_2), gw, t8 * 8, 8, 0)
            return carry
        lax.fori_loop(0, tb // 8, first, 0)
        wsc[1] = jnp.zeros(wsc.shape[1:], wsc.dtype)
        acc[...] = jnp.zeros_like(acc)

    _peer_build((rows_g, rows_1, rows_2), gw, tile * per_step, per_step, ((blk + 1) % 2) * buf_rows)

    h = h_ref[...]
    slot = g % 2
    base = (blk % 2) * buf_rows + tile * (half // nk)
    for s in range(half // nk):
        uc = jnp.concatenate([u_ref[0, s * nk:(s + 1) * nk, :], u_ref[1, s * nk:(s + 1) * nk, :]], axis=0)
        a = lax.dot_general(h, uc, (((1,), (1,)), ((), ())), preferred_element_type=F32)
        pk = gw[pl.ds(base + s, tb, stride=PEER_PITCH), :]
        gs = jnp.concatenate([lax.bitcast_convert_type(pk << 16, F32),
                              lax.bitcast_convert_type(pk & jnp.uint32(0xFFFF0000), F32)], axis=1)
        w = (gs * jax.nn.gelu(a)).astype(BF16)
        wsc[slot, :, s * nk:(s + 1) * nk] = w[:, :nk]
        wsc[slot, :, half + s * nk:half + (s + 1) * nk] = w[:, nk:]

    r = jnp.dot(wsc[1 - slot], v_ref[...].reshape(PEER_EXPERT_TILE, v_ref.shape[-1]), preferred_element_type=F32)
    acc[...] = acc[...] * jnp.where(ptile == 0, 0.0, 1.0) + r

    @pl.when((ptile == ntiles - 1) & (g > 0))
    def _():
        pb = (g - 1) // ntiles
        nctx = jnp.clip(lc - (pb % blocks_per_batch) * tb, 0, tb)
        rowi = lax.broadcasted_iota(I32, (tb, 1), 0)
        gate = jnp.where(rowi < nctx, mod_ref[0, 0][5:6], mod_ref[0, 1][5:6])
        o_ref[...] = x_ref[...] + gate * acc[...]


def _peer(gate, i1, i2, h2f, u_bf, v_bf, xf, modp, n, lc):
    nt, d = h2f.shape
    tb = PEER_TOKEN_BLOCK if n % PEER_TOKEN_BLOCK == 0 else TOKEN_BLOCK
    r = PEER_HEADS * PEER_TOPK
    te = PEER_EXPERT_TILE
    nblocks = nt // tb
    ntiles = PEER_EXPERTS // te
    bpb = n // tb
    u3 = u_bf.reshape(2, PEER_EXPERTS // 2, d)
    v3 = v_bf.reshape(2, PEER_EXPERTS // 2, d)

    def pblk(g):
        return jnp.maximum(g - 1, 0) // ntiles

    row0 = pl.BlockSpec((tb, r), lambda g: (0, 0))
    rown = pl.BlockSpec((tb, r), lambda g: (jnp.minimum(g // ntiles + 1, nblocks - 1), 0))
    return pl.pallas_call(
        functools.partial(_peer_kernel, nblocks=nblocks, ntiles=ntiles, blocks_per_batch=bpb, lc=lc),
        grid=(nblocks * ntiles + 1,),
        in_specs=[row0, row0, row0, rown, rown, rown,
                  pl.BlockSpec((tb, d), lambda g: (jnp.minimum(g // ntiles, nblocks - 1), 0)),
                  pl.BlockSpec((2, te // 2, d), lambda g: (0, g % ntiles, 0)),
                  pl.BlockSpec((2, te // 2, d), lambda g: (0, (g + ntiles - 1) % ntiles, 0)),
                  pl.BlockSpec((tb, d), lambda g: (pblk(g), 0)),
                  pl.BlockSpec((1, 2, 8, d), lambda g: (pblk(g) // bpb, 0, 0, 0))],
        out_specs=pl.BlockSpec((tb, d), lambda g: (pblk(g), 0)),
        out_shape=jax.ShapeDtypeStruct((nt, d), F32),
        scratch_shapes=[pltpu.VMEM((2 * tb * PEER_PITCH, PEER_N_KEYS), jnp.uint32),
                        pltpu.VMEM((2, tb, te), BF16),
                        pltpu.VMEM((tb, d), F32)],
        compiler_params=_cparams("arbitrary"),
        name="peer",
    )(gate, i1, i2, gate, i1, i2, h2f, u3, v3, xf, modp)


def _rope_tables(lc, l):
    t = jnp.arange(l)
    r = (t // GRID_W).astype(F32)
    col = (t % GRID_W).astype(F32)
    pairs = HEAD_DIM // 4
    inv = ROPE_BASE ** (-jnp.arange(pairs, dtype=F32) / pairs)
    ang_a = jnp.concatenate([r[:, None] * inv, col[:, None] * inv], axis=-1)
    half = RET_HEAD_DIM // 2
    inv_r = ROPE_BASE ** (-jnp.arange(half, dtype=F32) / half)
    ang_r = jnp.arange(l, dtype=F32)[:, None] * inv_r

    def tables(ang, heads):
        cos, sin = jnp.cos(ang), jnp.sin(ang)
        zero = jnp.zeros_like(sin)
        c = jnp.concatenate([cos, cos], -1)
        s1 = jnp.concatenate([-sin, zero], -1)
        s2 = jnp.concatenate([zero, sin], -1)
        w = c.shape[-1]
        ident = (jnp.ones((lc, w), F32), jnp.zeros((lc, w), F32), jnp.zeros((lc, w), F32))
        return tuple(jnp.tile(jnp.concatenate([i, x], 0), (1, heads)) for i, x in zip(ident, (c, s1, s2)))

    return tables(ang_a, N_HEADS) + tables(ang_r, RET_HEADS)


def kernel(x, c, ctx, c_ctx, ada_w, ada_b, norm1_w, norm2_w, w_in, w_out, q_norm_w, k_norm_w, attn_sink,
           s5_a_re, s5_a_im, s5_log_dt, s5_b_re, s5_b_im, s5_c_re, s5_c_im, s5_d, s5_glu_w, s5_glu_b,
           ret_decay, peer_wq, peer_keys, peer_u, peer_v):
    b, l, d = x.shape
    lc = ctx.shape[1]
    n = lc + l
    depth = ada_w.shape[0]
    tm = TOKEN_BLOCK
    ncb = lc // tm
    bpb = n // tm

    xa = jnp.concatenate([ctx, x], axis=1)
    rope_tabs = _rope_tables(lc, l)
    head_of = jnp.arange(ATT_Q_W) // HEAD_DIM
    bd = (head_of[:, None] == head_of[None, :]).astype(F32)
    crows = -(-(b + 1) // 8) * 8
    cpad = jnp.concatenate([c, c_ctx[None], jnp.zeros((crows - b - 1, d), F32)], axis=0)

    for layer in range(depth):
        mods = _adaln(cpad, ada_w[layer], ada_b[layer])
        mod = mods[:b].reshape(b, N_MOD, d)
        mod_c = jnp.broadcast_to(mods[b].reshape(1, N_MOD, d), (b, N_MOD, d))
        modp = jnp.pad(jnp.stack([mod_c, mod], axis=1), ((0, 0), (0, 0), (0, 8 - N_MOD), (0, 0)))

        z = _proj(xa, modp, norm1_w[layer], w_in[layer].astype(BF16), ncb)
        qnw = jnp.tile(q_norm_w[layer], N_HEADS).reshape(1, ATT_Q_W)
        knw = jnp.tile(k_norm_w[layer], N_KV_HEADS).reshape(1, ATT_KV_W)
        qa, ka, va, su, rq, rk, rv, rg = _prep(z, rope_tabs, qnw, knw, bd)

        att = _attention(attn_sink[layer].astype(F32), qa, ka, va, lc)
        mats = _s5_mats(s5_a_re[layer], s5_a_im[layer], s5_log_dt[layer], s5_b_re[layer], s5_b_im[layer],
                        s5_c_re[layer], s5_c_im[layer])
        ssm = _s5_group(su, mats, s5_d[layer], s5_glu_w[layer].astype(BF16), s5_glu_b[layer], lc)
        ret = _retention(rq, rk, rv, rg, _ret_tables(ret_decay[layer]), lc)

        xm, h2 = _mix(xa, att, ssm, ret, w_out[layer].astype(BF16), modp, norm2_w[layer], ncb)

        h2f = h2.reshape(b * n, d)
        keys = peer_keys[layer].reshape(2 * PEER_HEADS, PEER_N_KEYS, PEER_HALF).astype(BF16)
        gate, i1, i2 = _peer_select(h2f, peer_wq[layer].T.astype(BF16), keys)
        xf = _peer(gate, i1, i2, h2f, peer_u[layer].astype(BF16), peer_v[layer].astype(BF16),
                   xm.reshape(b * n, d), modp, n, lc)
        xa = xf.reshape(b, n, d)

    return xa[:, lc:]
```

```python
import functools
import math

import jax
import jax.numpy as jnp
from jax import lax
from jax.experimental import pallas as pl
from jax.experimental.pallas import tpu as pltpu

F32 = jnp.float32
BF16 = jnp.bfloat16
I32 = jnp.int32
HI = lax.Precision.HIGHEST

D_MODEL = 1024
N_MOD = 6
EPS = 1e-6
GRID_W = 64

N_HEADS = 8
N_KV_HEADS = 2
GQA = N_HEADS // N_KV_HEADS
HEAD_DIM = 64
ATT_BLOCK = 128
ROPE_BASE = 10000.0
ATT_Q_W = N_HEADS * HEAD_DIM
ATT_KV_W = N_KV_HEADS * HEAD_DIM

S5_GROUPS = 16
S5_GROUP_CH = 16
S5_STATE = 64
S5_W = S5_GROUPS * S5_GROUP_CH
S5_CHUNK = 32

RET_HEADS = 4
RET_HEAD_DIM = 64
RET_W = RET_HEADS * RET_HEAD_DIM
RET_CHUNK = 128

MIX_W = ATT_Q_W + S5_W + RET_W
PROJ_W = ATT_Q_W + 2 * ATT_KV_W + S5_W + 4 * RET_W

PEER_HEADS = 8
PEER_N_KEYS = 128
PEER_EXPERTS = PEER_N_KEYS * PEER_N_KEYS
PEER_QUERY_DIM = 128
PEER_HALF = PEER_QUERY_DIM // 2
PEER_TOPK = 16

TOKEN_BLOCK = 256
PEER_TOKEN_BLOCK = 384
PEER_EXPERT_TILE = 1024
PEER_PITCH = PEER_N_KEYS // 2 + 8
VMEM_LIMIT = 56 * 1024 * 1024


def _cparams(*sem):
    return pltpu.CompilerParams(dimension_semantics=sem, vmem_limit_bytes=VMEM_LIMIT)


def _adaln_kernel(c_ref, w_ref, b_ref, o_ref):
    c = c_ref[...]
    s = c * jax.nn.sigmoid(c)
    o_ref[...] = jnp.dot(s, w_ref[...], precision=HI, preferred_element_type=F32) + b_ref[...]


def _adaln(cpad, w, b):
    rows, d = cpad.shape
    n = w.shape[1]
    tn = 1536
    return pl.pallas_call(
        _adaln_kernel,
        grid=(n // tn,),
        in_specs=[pl.BlockSpec((rows, d), lambda j: (0, 0)),
                  pl.BlockSpec((d, tn), lambda j: (0, j)),
                  pl.BlockSpec((1, tn), lambda j: (0, j))],
        out_specs=pl.BlockSpec((rows, tn), lambda j: (0, j)),
        out_shape=jax.ShapeDtypeStruct((rows, n), F32),
        compiler_params=_cparams("arbitrary"),
        name="adaln",
    )(cpad, w, b.reshape(1, n))


def _proj_kernel(x_ref, mod_ref, nw_ref, w_ref, z_ref):
    x = x_ref[0]
    ms = jnp.mean(x * x, axis=-1, keepdims=True)
    y = x * lax.rsqrt(ms + EPS) * nw_ref[...]
    m = mod_ref[0, 0]
    h = y * (1.0 + m[1:2]) + m[0:1]
    z_ref[0] = jnp.dot(h.astype(BF16), w_ref[...], preferred_element_type=F32)


def _proj(xa, modp, nw, w_bf, ncb):
    b, n, d = xa.shape
    tm = TOKEN_BLOCK
    pw = w_bf.shape[1]
    return pl.pallas_call(
        _proj_kernel,
        grid=(b, n // tm),
        in_specs=[pl.BlockSpec((1, tm, d), lambda i, j: (i, j, 0)),
                  pl.BlockSpec((1, 1, 8, d), lambda i, j: (i, jnp.minimum(j // ncb, 1), 0, 0)),
                  pl.BlockSpec((1, d), lambda i, j: (0, 0)),
                  pl.BlockSpec((d, pw), lambda i, j: (0, 0))],
        out_specs=pl.BlockSpec((1, tm, pw), lambda i, j: (i, j, 0)),
        out_shape=jax.ShapeDtypeStruct((b, n, pw), F32),
        compiler_params=_cparams("arbitrary", "arbitrary"),
        name="proj",
    )(xa, modp, nw.reshape(1, d), w_bf)


def _rope(x, c, s1, s2):
    w = x.shape[-1]
    return x * c + pltpu.roll(x, w - 32, 1) * s1 + pltpu.roll(x, 32, 1) * s2


def _prep_kernel(z_ref, cq_ref, s1q_ref, s2q_ref, cr_ref, s1r_ref, s2r_ref, qnw_ref, knw_ref, bd_ref,
                 qa_ref, ka_ref, va_ref, su_ref, rq_ref, rk_ref, rv_ref, rg_ref):
    o = 0
    aq = z_ref[0, :, o:o + ATT_Q_W]; o += ATT_Q_W
    ak = z_ref[0, :, o:o + ATT_KV_W]; o += ATT_KV_W
    av = z_ref[0, :, o:o + ATT_KV_W]; o += ATT_KV_W
    su = z_ref[0, :, o:o + S5_W]; o += S5_W
    rq = z_ref[0, :, o:o + RET_W]; o += RET_W
    rk = z_ref[0, :, o:o + RET_W]; o += RET_W
    rv = z_ref[0, :, o:o + RET_W]; o += RET_W
    rg = z_ref[0, :, o:o + RET_W]

    bd = bd_ref[...]
    inv = 1.0 / HEAD_DIM
    msq = jnp.dot(aq * aq, bd, precision=HI, preferred_element_type=F32) * inv
    qn = aq * lax.rsqrt(msq + EPS) * qnw_ref[...]
    msk = jnp.dot(ak * ak, bd[:ATT_KV_W, :ATT_KV_W], precision=HI, preferred_element_type=F32) * inv
    kn = ak * lax.rsqrt(msk + EPS) * knw_ref[...]

    cq, s1q, s2q = cq_ref[...], s1q_ref[...], s2q_ref[...]
    qa_ref[0] = (_rope(qn, cq, s1q, s2q) * (HEAD_DIM ** -0.5)).astype(BF16)
    ka_ref[0] = _rope(kn, cq[:, :ATT_KV_W], s1q[:, :ATT_KV_W], s2q[:, :ATT_KV_W]).astype(BF16)
    va_ref[0] = av.astype(BF16)
    su_ref[0] = su
    cr, s1r, s2r = cr_ref[...], s1r_ref[...], s2r_ref[...]
    rq_ref[0] = (_rope(rq, cr, s1r, s2r) * (RET_HEAD_DIM ** -0.5)).astype(BF16)
    rk_ref[0] = _rope(rk, cr, s1r, s2r).astype(BF16)
    rv_ref[0] = rv.astype(BF16)
    rg_ref[0] = rg


def _prep(z, tabs, qnw, knw, bd):
    b, n, pw = z.shape
    tm = TOKEN_BLOCK
    cq, s1q, s2q, cr, s1r, s2r = tabs

    def tok(w):
        return pl.BlockSpec((1, tm, w), lambda j, i: (i, j, 0))

    def tab(w):
        return pl.BlockSpec((tm, w), lambda j, i: (j, 0))

    def full(shape):
        return pl.BlockSpec(shape, lambda j, i: (0,) * len(shape))

    widths = (ATT_Q_W, ATT_KV_W, ATT_KV_W, S5_W, RET_W, RET_W, RET_W, RET_W)
    dtypes = (BF16, BF16, BF16, F32, BF16, BF16, BF16, F32)
    return pl.pallas_call(
        _prep_kernel,
        grid=(n // tm, b),
        in_specs=[tok(pw), tab(ATT_Q_W), tab(ATT_Q_W), tab(ATT_Q_W), tab(RET_W), tab(RET_W), tab(RET_W),
                  full((1, ATT_Q_W)), full((1, ATT_KV_W)), full((ATT_Q_W, ATT_Q_W))],
        out_specs=[tok(w) for w in widths],
        out_shape=[jax.ShapeDtypeStruct((b, n, w), dt) for w, dt in zip(widths, dtypes)],
        compiler_params=_cparams("arbitrary", "arbitrary"),
        name="prep",
    )(z, cq, s1q, s2q, cr, s1r, s2r, qnw, knw, bd)


def _attn_kernel(sink_ref, q_ref, k_ref, v_ref, o_ref, *, lc, l):
    blk = ATT_BLOCK
    nb = l // blk
    n = pl.program_id(1) - lc // blk
    ctx_off = jnp.where(n >= 0, 0, -(1 << 20))

    def kstart(m):
        return pl.multiple_of(lc + jnp.clip(m, 0, nb - 1) * blk, blk)

    starts = (kstart(n - 1), kstart(n), kstart(n + 1))
    rows = GQA * blk
    row = lax.broadcasted_iota(I32, (rows, 3 * blk), 0)
    col = lax.broadcasted_iota(I32, (rows, 3 * blk), 1)
    qi = row & (blk - 1)
    dist = col - blk - qi
    kpos = (n - 1) * blk + ctx_off + col
    valid = (jnp.abs(dist) <= blk) & (kpos >= 0) & (kpos < l)
    grow = lax.broadcasted_iota(I32, (rows, 1), 0) // blk

    for hkv in range(N_KV_HEADS):
        q4 = jnp.concatenate(
            [q_ref[0, :, (hkv * GQA + g) * HEAD_DIM:(hkv * GQA + g + 1) * HEAD_DIM] for g in range(GQA)], axis=0)
        ksl = slice(hkv * HEAD_DIM, (hkv + 1) * HEAD_DIM)
        kcat = jnp.concatenate([k_ref[0, pl.ds(s, blk), ksl] for s in starts] + [k_ref[0, 0:lc, ksl]], axis=0)
        vcat = jnp.concatenate([v_ref[0, pl.ds(s, blk), ksl] for s in starts] + [v_ref[0, 0:lc, ksl]], axis=0)
        s = lax.dot_general(q4, kcat, (((1,), (1,)), ((), ())), preferred_element_type=F32)
        s_win = jnp.where(valid, s[:, :3 * blk], -jnp.inf)
        s_ctx = s[:, 3 * blk:]
        sink = jnp.zeros((rows, 1), F32)
        for g in range(GQA):
            sink = jnp.where(grow == g, sink_ref[hkv * GQA + g], sink)
        m = jnp.maximum(jnp.maximum(jnp.max(s_win, axis=-1, keepdims=True),
                                    jnp.max(s_ctx, axis=-1, keepdims=True)), sink)
        e_win = jnp.exp(s_win - m)
        e_ctx = jnp.exp(s_ctx - m)
        den = (jnp.sum(e_win, axis=-1, keepdims=True) + jnp.sum(e_ctx, axis=-1, keepdims=True)
               + jnp.exp(sink - m))
        p = jnp.concatenate([e_win, e_ctx], axis=1).astype(BF16)
        o = jnp.dot(p, vcat, preferred_element_type=F32) / den
        for g in range(GQA):
            c0 = (hkv * GQA + g) * HEAD_DIM
            o_ref[0, :, c0:c0 + HEAD_DIM] = o[g * blk:(g + 1) * blk].astype(o_ref.dtype)


def _attention(sink, qa, ka, va, lc):
    b, n, _ = qa.shape
    return pl.pallas_call(
        functools.partial(_attn_kernel, lc=lc, l=n - lc),
        grid=(b, n // ATT_BLOCK),
        in_specs=[pl.BlockSpec(memory_space=pltpu.SMEM),
                  pl.BlockSpec((1, ATT_BLOCK, ATT_Q_W), lambda i, j: (i, j, 0)),
                  pl.BlockSpec((1, n, ATT_KV_W), lambda i, j: (i, 0, 0)),
                  pl.BlockSpec((1, n, ATT_KV_W), lambda i, j: (i, 0, 0))],
        out_specs=pl.BlockSpec((1, ATT_BLOCK, ATT_Q_W), lambda i, j: (i, j, 0)),
        out_shape=jax.ShapeDtypeStruct((b, n, ATT_Q_W), BF16),
        compiler_params=_cparams("arbitrary", "arbitrary"),
        name="attention",
    )(sink, qa, ka, va)


def _s5_kernel(u_ref, m_ref, bf_ref, bb_ref, vf_ref, vb_ref, af_ref, ab_ref, y_ref,
               incf, incb, hinf, hinb, *, nchunk, nchunk_ctx, nb):
    u = u_ref[0]
    incf[...] = jnp.dot(u, bf_ref[0], precision=HI, preferred_element_type=F32)
    incb[...] = jnp.dot(u, bb_ref[0], precision=HI, preferred_element_type=F32)
    af = af_ref[0]
    ab = ab_ref[0]
    zero = jnp.zeros((nb, 2 * S5_STATE), F32)

    def advance(h, a):
        return h * a[0:1] + pltpu.roll(h, S5_STATE, 1) * a[1:2]

    def fwd(c, h):
        r = pl.multiple_of(c * nb, nb)
        hinf[pl.ds(r, nb), :] = h
        return advance(h, af) + incf[pl.ds(r, nb), :]

    lax.fori_loop(0, nchunk, fwd, zero)

    def bwd(c, h):
        r = pl.multiple_of(c * nb, nb)
        hinb[pl.ds(r, nb), :] = h
        return advance(h, ab) + incb[pl.ds(r, nb), :]

    h = lax.fori_loop(0, nchunk_ctx, lambda i, h: bwd(nchunk_ctx - 1 - i, h), zero)
    lax.fori_loop(0, nchunk - nchunk_ctx, lambda i, h: bwd(nchunk - 1 - i, h), h)

    y_ref[0] = (jnp.dot(u, m_ref[0], precision=HI, preferred_element_type=F32)
                + jnp.dot(hinf[...], vf_ref[0], precision=HI, preferred_element_type=F32)
                + jnp.dot(hinb[...], vb_ref[0], precision=HI, preferred_element_type=F32))


def _s5_scan(ug, mats, nchunk, nchunk_ctx, nb):
    g, r, cw = ug.shape
    m, bf, bb, vf, vb, af, ab = mats
    sw = 2 * S5_STATE

    def blk(shape):
        return pl.BlockSpec((1,) + shape, lambda i: (i, 0, 0))

    return pl.pallas_call(
        functools.partial(_s5_kernel, nchunk=nchunk, nchunk_ctx=nchunk_ctx, nb=nb),
        grid=(g,),
        in_specs=[blk((r, cw)), blk((cw, cw)), blk((cw, sw)), blk((cw, sw)), blk((sw, cw)), blk((sw, cw)),
                  blk((2, sw)), blk((2, sw))],
        out_specs=blk((r, cw)),
        out_shape=jax.ShapeDtypeStruct((g, r, cw), F32),
        scratch_shapes=[pltpu.VMEM((r, sw), F32)] * 4,
        compiler_params=_cparams("arbitrary"),
        name="s5_scan",
    )(ug, m, bf, bb, vf, vb, af, ab)


def _s5_out_kernel(y_ref, u_ref, d_ref, w_ref, b_ref, o_ref):
    y = jax.nn.gelu(y_ref[0] + d_ref[...] * u_ref[0])
    gate = jax.nn.sigmoid(jnp.dot(y.astype(BF16), w_ref[...], preferred_element_type=F32) + b_ref[...])
    o_ref[0] = (y * gate).astype(o_ref.dtype)


def _s5_out(y, u, d, w_bf, bvec):
    b, n, w = y.shape
    tm = TOKEN_BLOCK
    tok = pl.BlockSpec((1, tm, w), lambda i, j: (i, j, 0))
    vec = pl.BlockSpec((1, w), lambda i, j: (0, 0))
    return pl.pallas_call(
        _s5_out_kernel,
        grid=(b, n // tm),
        in_specs=[tok, tok, vec, pl.BlockSpec((w, w), lambda i, j: (0, 0)), vec],
        out_specs=tok,
        out_shape=jax.ShapeDtypeStruct((b, n, w), BF16),
        compiler_params=_cparams("arbitrary", "arbitrary"),
        name="s5_out",
    )(y, u, d.reshape(1, w), w_bf, bvec.reshape(1, w))


def _s5_mats(a_re, a_im, log_dt, b_re, b_im, c_re, c_im):
    t = S5_CHUNK
    lam = lax.complex(a_re.astype(F32), a_im.astype(F32))
    ldt = lam * jnp.exp(log_dt.astype(F32))
    a_bar = jnp.exp(ldt)
    b_bar = ((a_bar - 1.0) / lam)[..., None] * lax.complex(b_re.astype(F32), b_im.astype(F32))
    cm = lax.complex(c_re.astype(F32), c_im.astype(F32))
    steps = jnp.arange(t + 1, dtype=F32)
    pw = jnp.exp(ldt[:, None] * steps[None, :, None, None])
    kern = jnp.real(jnp.einsum('dgip,dkgp,dgpj->dkgij', cm, pw, b_bar, precision=HI))

    s_idx = jnp.arange(t)[:, None]
    t_idx = jnp.arange(t)[None, :]
    dist = t_idx - s_idx
    mf = jnp.where((dist >= 0)[:, :, None, None, None], kern[0][jnp.clip(dist, 0, t)], 0.0)
    mb = jnp.where((dist <= 0)[:, :, None, None, None], kern[1][jnp.clip(-dist, 0, t)], 0.0)
    m = jnp.transpose(mf + mb, (2, 0, 4, 1, 3)).reshape(S5_GROUPS, t * S5_GROUP_CH, t * S5_GROUP_CH)

    def split(zc, axis):
        return jnp.concatenate([jnp.real(zc), jnp.imag(zc)], axis=axis)

    bfc = pw[0, t - 1 - jnp.arange(t)][:, :, :, None] * b_bar[0][None]
    bbc = pw[1, jnp.arange(t)][:, :, :, None] * b_bar[1][None]
    bf = split(jnp.transpose(bfc, (1, 0, 3, 2)), 3).reshape(S5_GROUPS, t * S5_GROUP_CH, 2 * S5_STATE)
    bb = split(jnp.transpose(bbc, (1, 0, 3, 2)), 3).reshape(S5_GROUPS, t * S5_GROUP_CH, 2 * S5_STATE)
    vfc = cm[0][:, None] * pw[0, 1 + jnp.arange(t)].transpose(1, 0, 2)[:, :, None, :]
    vbc = cm[1][:, None] * pw[1, t - jnp.arange(t)].transpose(1, 0, 2)[:, :, None, :]

    def readout(zc):
        zt = jnp.transpose(zc, (0, 3, 1, 2))
        return jnp.concatenate([jnp.real(zt), -jnp.imag(zt)], axis=1).reshape(
            S5_GROUPS, 2 * S5_STATE, t * S5_GROUP_CH)

    def decay(ac):
        re, im = jnp.real(ac), jnp.imag(ac)
        return jnp.stack([jnp.concatenate([re, re], -1), jnp.concatenate([-im, im], -1)], axis=1)

    return m, bf, bb, readout(vfc), readout(vbc), decay(pw[0, t]), decay(pw[1, t])


def _s5_group(su, mats, d, glu_w_bf, glu_b, lc):
    b, n, _ = su.shape
    t = S5_CHUNK
    nchunk = n // t
    bp = -(-b // 8) * 8
    up = jnp.pad(su, ((0, bp - b), (0, 0), (0, 0)))
    ug = up.reshape(bp, nchunk, t, S5_GROUPS, S5_GROUP_CH).transpose(3, 1, 0, 2, 4)
    ug = ug.reshape(S5_GROUPS, nchunk * bp, t * S5_GROUP_CH)
    yg = _s5_scan(ug, mats, nchunk, lc // t, bp)
    y = yg.reshape(S5_GROUPS, nchunk, bp, t, S5_GROUP_CH).transpose(2, 1, 3, 0, 4).reshape(bp, n, S5_W)[:b]
    return _s5_out(y, su, d, glu_w_bf, glu_b)


def _ret_kernel(cd_ref, q_ref, k_ref, v_ref, g_ref, dm_ref, qwf_ref, kwf_ref, qwb_ref, kwb_ref, o_ref,
                sf_store, sf, sb, *, nchunk, nchunk_ctx):
    ck = RET_CHUNK
    hd = RET_HEAD_DIM
    tdims = (((0,), (0,)), ((), ()))

    def rows(c):
        return pl.ds(pl.multiple_of(c * ck, ck), ck)

    sf[...] = jnp.zeros_like(sf)
    sb[...] = jnp.zeros_like(sb)

    def fwd(c, carry):
        kc = k_ref[0, rows(c), :].astype(F32) * kwf_ref[...]
        vc = v_ref[0, rows(c), :]
        for h in range(RET_HEADS):
            hs = slice(h * hd, (h + 1) * hd)
            sf_store[c, h] = sf[h]
            kv = lax.dot_general(kc[:, hs].astype(BF16), vc[:, hs], tdims, preferred_element_type=F32)
            sf[h] = cd_ref[0, h] * sf[h] + kv
        return carry

    lax.fori_loop(0, nchunk, fwd, 0)

    def bwd(c, carry):
        qc = q_ref[0, rows(c), :].astype(F32)
        kc_b = k_ref[0, rows(c), :]
        vc = v_ref[0, rows(c), :]
        gc = g_ref[0, rows(c), :]
        qf = (qc * qwf_ref[...]).astype(BF16)
        qb = (qc * qwb_ref[...]).astype(BF16)
        kb = (kc_b.astype(F32) * kwb_ref[...]).astype(BF16)
        q16 = q_ref[0, rows(c), :]
        for h in range(RET_HEADS):
            hs = slice(h * hd, (h + 1) * hd)
            sc = lax.dot_general(q16[:, hs], kc_b[:, hs], (((1,), (1,)), ((), ())), preferred_element_type=F32)
            sc = (sc * dm_ref[h]).astype(BF16)
            o = jnp.dot(sc, vc[:, hs], preferred_element_type=F32)
            o = o + jnp.dot(qf[:, hs], sf_store[c, h].astype(BF16), preferred_element_type=F32)
            o = o + jnp.dot(qb[:, hs], sb[h].astype(BF16), preferred_element_type=F32)
            o = o * lax.rsqrt(jnp.mean(o * o, axis=-1, keepdims=True) + EPS)
            gh = gc[:, hs]
            o_ref[0, rows(c), hs] = (o * (gh * jax.nn.sigmoid(gh))).astype(o_ref.dtype)
            kv = lax.dot_general(kb[:, hs], vc[:, hs], tdims, preferred_element_type=F32)
            sb[h] = cd_ref[1, h] * sb[h] + kv
        return carry

    lax.fori_loop(0, nchunk_ctx, lambda i, cr: bwd(nchunk_ctx - 1 - i, cr), 0)
    lax.fori_loop(0, nchunk - nchunk_ctx, lambda i, cr: bwd(nchunk - 1 - i, cr), 0)


def _retention(rq, rk, rv, rg, tabs, lc):
    b, n, w = rq.shape
    cd, dm, qwf, kwf, qwb, kwb = tabs
    nchunk = n // RET_CHUNK
    tok = pl.BlockSpec((1, n, w), lambda i: (i, 0, 0))
    tab = pl.BlockSpec((RET_CHUNK, w), lambda i: (0, 0))
    st = pltpu.VMEM((RET_HEADS, RET_HEAD_DIM, RET_HEAD_DIM), F32)
    return pl.pallas_call(
        functools.partial(_ret_kernel, nchunk=nchunk, nchunk_ctx=lc // RET_CHUNK),
        grid=(b,),
        in_specs=[pl.BlockSpec(memory_space=pltpu.SMEM), tok, tok, tok, tok,
                  pl.BlockSpec((RET_HEADS, RET_CHUNK, RET_CHUNK), lambda i: (0, 0, 0)), tab, tab, tab, tab],
        out_specs=tok,
        out_shape=jax.ShapeDtypeStruct((b, n, w), BF16),
        scratch_shapes=[pltpu.VMEM((nchunk, RET_HEADS, RET_HEAD_DIM, RET_HEAD_DIM), F32), st, st],
        compiler_params=_cparams("arbitrary"),
        name="retention",
    )(cd, rq, rk, rv, rg, dm, qwf, kwf, qwb, kwb)


def _ret_tables(decay_logit):
    lg = jax.nn.log_sigmoid(decay_logit.astype(F32))
    pos = jnp.arange(RET_CHUNK, dtype=F32)
    rel = pos[:, None] - pos[None, :]
    dm = jnp.where(rel > 0, jnp.exp(lg[0][:, None, None] * jnp.maximum(rel, 0.0)),
                   jnp.where(rel < 0, jnp.exp(lg[1][:, None, None] * jnp.maximum(-rel, 0.0)), 2.0))

    def wide(e):
        return jnp.repeat(e.T, RET_HEAD_DIM, axis=1)

    qwf = wide(jnp.exp(lg[0][:, None] * (pos + 1.0)))
    kwf = wide(jnp.exp(lg[0][:, None] * (RET_CHUNK - 1.0 - pos)))
    qwb = wide(jnp.exp(lg[1][:, None] * (RET_CHUNK - pos)))
    kwb = wide(jnp.exp(lg[1][:, None] * pos))
    cd = jnp.exp(lg * RET_CHUNK)
    return cd, dm, qwf, kwf, qwb, kwb


def _mix_kernel(x_ref, att_ref, ssm_ref, ret_ref, w_ref, mod_ref, nw_ref, xo_ref, h2_ref):
    m = mod_ref[0, 0]
    y = jnp.dot(att_ref[0], w_ref[0:ATT_Q_W, :], preferred_element_type=F32)
    y = y + jnp.dot(ssm_ref[0], w_ref[ATT_Q_W:ATT_Q_W + S5_W, :], preferred_element_type=F32)
    y = y + jnp.dot(ret_ref[0], w_ref[ATT_Q_W + S5_W:MIX_W, :], preferred_element_type=F32)
    x = x_ref[0] + m[2:3] * y
    xo_ref[0] = x
    ms = jnp.mean(x * x, axis=-1, keepdims=True)
    h = x * lax.rsqrt(ms + EPS) * nw_ref[...]
    h2_ref[0] = (h * (1.0 + m[4:5]) + m[3:4]).astype(BF16)


def _mix(xa, att, ssm, ret, w_bf, modp, nw, ncb):
    b, n, d = xa.shape
    tm = TOKEN_BLOCK

    def tok(w):
        return pl.BlockSpec((1, tm, w), lambda i, j: (i, j, 0))

    return pl.pallas_call(
        _mix_kernel,
        grid=(b, n // tm),
        in_specs=[tok(d), tok(ATT_Q_W), tok(S5_W), tok(RET_W),
                  pl.BlockSpec((MIX_W, d), lambda i, j: (0, 0)),
                  pl.BlockSpec((1, 1, 8, d), lambda i, j: (i, jnp.minimum(j // ncb, 1), 0, 0)),
                  pl.BlockSpec((1, d), lambda i, j: (0, 0))],
        out_specs=[tok(d), tok(d)],
        out_shape=[jax.ShapeDtypeStruct((b, n, d), F32), jax.ShapeDtypeStruct((b, n, d), BF16)],
        compiler_params=_cparams("arbitrary", "arbitrary"),
        name="mix",
    )(xa, att, ssm, ret, w_bf, modp, nw.reshape(1, d))


def _peer_select_kernel(h_ref, wqt_ref, keys_ref, gate_ref, i1_ref, i2_ref,
                        s_sc, t_sc, i_sc, best_sc, sel_sc, g_all, e1_all, e2_all):
    tb = h_ref.shape[0]
    kk = PEER_TOPK
    nk = PEER_N_KEYS
    ninf = -jnp.inf
    qt = lax.dot_general(wqt_ref[...], h_ref[...], (((1,), (1,)), ((), ())), preferred_element_type=F32)
    qt = qt.astype(BF16)
    for hh in range(2 * PEER_HEADS):
        h, half = divmod(hh, 2)
        r0 = h * PEER_QUERY_DIM + half * PEER_HALF
        s_sc[hh] = jnp.dot(keys_ref[half * PEER_HEADS + h], qt[r0:r0 + PEER_HALF],
                           preferred_element_type=F32)

    kiota = lax.broadcasted_iota(I32, (nk, tb), 0)

    def stage1(hh, carry):
        s = s_sc[hh]
        for it in range(kk):
            m = jnp.max(s, axis=0, keepdims=True)
            idx = jnp.min(jnp.where(s == m, kiota, nk), axis=0, keepdims=True)
            t_sc[hh, it:it + 1, :] = m
            i_sc[hh, it:it + 1, :] = idx
            s = jnp.where(kiota == idx, ninf, s)
        return carry

    lax.fori_loop(0, 2 * PEER_HEADS, stage1, 0)

    ncand = 80
    rho = lax.broadcasted_iota(I32, (ncand, tb), 0)
    tile = rho >> 3
    wi = rho & 7
    ca = jnp.where(tile <= 1, 0, jnp.where(tile == 9, 8 + wi, tile - 1))
    cb = jnp.where(tile == 1, 8 + wi, jnp.where(tile == 9, 0, wi))
    flat = ca * kk + cb
    cvalid = (ca + 1) * (cb + 1) <= kk

    def stage2(h, carry):
        t1 = t_sc[2 * h]
        t2 = t_sc[2 * h + 1]
        tiles = [t1[0:1] + t2[0:8], t1[0:1] + t2[8:16]]
        tiles += [t1[a:a + 1] + t2[0:8] for a in range(1, 8)]
        tiles += [t1[8:16] + t2[0:1]]
        cand = jnp.where(cvalid, jnp.concatenate(tiles, axis=0), ninf)
        for it in range(kk):
            m = jnp.max(cand, axis=0, keepdims=True)
            idx = jnp.min(jnp.where(cand == m, flat, kk * kk), axis=0, keepdims=True)
            best_sc[it:it + 1, :] = m
            sel_sc[it:it + 1, :] = idx
            cand = jnp.where(flat == idx, ninf, cand)
        best = best_sc[...]
        sel = sel_sc[...]
        asel = sel >> 4
        bsel = sel & (kk - 1)
        i1 = i_sc[2 * h]
        i2 = i_sc[2 * h + 1]
        e1 = jnp.zeros((kk, tb), I32)
        e2 = jnp.zeros((kk, tb), I32)
        for a in range(kk):
            e1 = jnp.where(asel == a, i1[a:a + 1], e1)
            e2 = jnp.where(bsel == a, i2[a:a + 1], e2)
        ex = jnp.exp(best - best[0:1])
        gate = ex / jnp.sum(ex, axis=0, keepdims=True)
        r = pl.ds(pl.multiple_of(h * kk, kk), kk)
        g_all[r, :] = gate
        e1_all[r, :] = e1
        e2_all[r, :] = e2
        return carry

    lax.fori_loop(0, PEER_HEADS, stage2, 0)
    gate_ref[...] = g_all[...].T
    i1_ref[...] = e1_all[...].T
    i2_ref[...] = e2_all[...].T


def _peer_select(h2f, wqt_bf, keys_bf):
    nt, d = h2f.shape
    tb = TOKEN_BLOCK
    r = PEER_HEADS * PEER_TOPK
    row = pl.BlockSpec((tb, r), lambda i: (i, 0))
    return pl.pallas_call(
        _peer_select_kernel,
        grid=(nt // tb,),
        in_specs=[pl.BlockSpec((tb, d), lambda i: (i, 0)),
                  pl.BlockSpec((d, d), lambda i: (0, 0)),
                  pl.BlockSpec((2 * PEER_HEADS, PEER_N_KEYS, PEER_HALF), lambda i: (0, 0, 0))],
        out_specs=[row, row, row],
        out_shape=[jax.ShapeDtypeStruct((nt, r), F32), jax.ShapeDtypeStruct((nt, r), I32),
                   jax.ShapeDtypeStruct((nt, r), I32)],
        scratch_shapes=[pltpu.VMEM((2 * PEER_HEADS, PEER_N_KEYS, tb), F32),
                        pltpu.VMEM((2 * PEER_HEADS, PEER_TOPK, tb), F32),
                        pltpu.VMEM((2 * PEER_HEADS, PEER_TOPK, tb), I32),
                        pltpu.VMEM((PEER_TOPK, tb), F32),
                        pltpu.VMEM((PEER_TOPK, tb), I32),
                        pltpu.VMEM((r, tb), F32),
                        pltpu.VMEM((r, tb), I32),
                        pltpu.VMEM((r, tb), I32)],
        compiler_params=_cparams("arbitrary"),
        name="peer_select",
    )(h2f, wqt_bf, keys_bf)


def _peer_build(rows, dst, row0, count, base):
    gate_ref, i1_ref, i2_ref = rows
    nk = PEER_N_KEYS
    siota = lax.broadcasted_iota(I32, (nk, nk), 0)
    for g8 in range(count // 8):
        r0 = row0 + g8 * 8
        if not isinstance(r0, int):
            r0 = pl.multiple_of(r0, 8)
        i1t = i1_ref[pl.ds(r0, 8), :]
        i2t = i2_ref[pl.ds(r0, 8), :]
        gt = gate_ref[pl.ds(r0, 8), :]
        for k in range(8):
            pt = jnp.where(siota == i1t[k:k + 1], gt[k:k + 1], 0.0).astype(BF16)
            qt = jnp.where(siota == i2t[k:k + 1], 1.0, 0.0)
            g = jnp.dot(pt, qt.T.astype(BF16), preferred_element_type=F32)
            gb = lax.bitcast_convert_type(g.astype(BF16).astype(F32), jnp.uint32)
            pk = (gb[:nk // 2] >> 16) | gb[nk // 2:]
            dst[pl.ds(pl.multiple_of(base + (r0 + k) * PEER_PITCH, 8), nk // 2), :] = pk


def _peer_kernel(rows0_g, rows0_1, rows0_2, rows_g, rows_1, rows_2, h_ref, ua_ref, ub_ref, v_ref, x_ref, mod_ref,
                 o_ref,
                 gw, wsc, acc, *, nblocks, ntiles, blocks_per_batch, lc):
    tb = h_ref.shape[0]
    nk = PEER_N_KEYS
    half = PEER_EXPERT_TILE // 2
    per_step = tb // ntiles
    buf_rows = tb * PEER_PITCH
    g = pl.program_id(0)
    blk = jnp.minimum(g // ntiles, nblocks - 1)
    tile = g % ntiles
    ptile = (g + ntiles - 1) % ntiles

    @pl.when(g == 0)
    def _():
        def first(t8, carry):
            _peer_build((rows0_g, rows0_1, rows0_2), gw, t8 * 8, 8, 0)
            return carry
        lax.fori_loop(0, tb // 8, first, 0)
        wsc[1] = jnp.zeros(wsc.shape[1:], wsc.dtype)
        acc[...] = jnp.zeros_like(acc)

    _peer_build((rows_g, rows_1, rows_2), gw, tile * per_step, per_step, ((blk + 1) % 2) * buf_rows)

    h = h_ref[...]
    slot = g % 2
    base = (blk % 2) * buf_rows + tile * (half // nk)
    for s in range(half // nk):
        uc = jnp.concatenate([ua_ref[:, s * nk:(s + 1) * nk], ub_ref[:, s * nk:(s + 1) * nk]], axis=1)
        a = jnp.dot(h, uc, preferred_element_type=F32)
        pk = gw[pl.ds(base + s, tb, stride=PEER_PITCH), :]
        gs = jnp.concatenate([lax.bitcast_convert_type(pk << 16, F32),
                              lax.bitcast_convert_type(pk & jnp.uint32(0xFFFF0000), F32)], axis=1)
        w = (gs * jax.nn.gelu(a)).astype(BF16)
        wsc[slot, :, s * nk:(s + 1) * nk] = w[:, :nk]
        wsc[slot, :, half + s * nk:half + (s + 1) * nk] = w[:, nk:]

    r = jnp.dot(wsc[1 - slot], v_ref[...].reshape(PEER_EXPERT_TILE, v_ref.shape[-1]), preferred_element_type=F32)
    acc[...] = acc[...] * jnp.where(ptile == 0, 0.0, 1.0) + r

    @pl.when((ptile == ntiles - 1) & (g > 0))
    def _():
        pb = (g - 1) // ntiles
        nctx = jnp.clip(lc - (pb % blocks_per_batch) * tb, 0, tb)
        rowi = lax.broadcasted_iota(I32, (tb, 1), 0)
        gate = jnp.where(rowi < nctx, mod_ref[0, 0][5:6], mod_ref[0, 1][5:6])
        o_ref[...] = x_ref[...] + gate * acc[...]


def _peer(gate, i1, i2, h2f, ut_bf, v_bf, xf, modp, n, lc):
    nt, d = h2f.shape
    tb = PEER_TOKEN_BLOCK if n % PEER_TOKEN_BLOCK == 0 else TOKEN_BLOCK
    r = PEER_HEADS * PEER_TOPK
    te = PEER_EXPERT_TILE
    nblocks = nt // tb
    ntiles = PEER_EXPERTS // te
    bpb = n // tb
    v3 = v_bf.reshape(2, PEER_EXPERTS // 2, d)
    hb = (PEER_EXPERTS // 2) // (te // 2)

    def pblk(g):
        return jnp.maximum(g - 1, 0) // ntiles

    row0 = pl.BlockSpec((tb, r), lambda g: (0, 0))
    rown = pl.BlockSpec((tb, r), lambda g: (jnp.minimum(g // ntiles + 1, nblocks - 1), 0))
    return pl.pallas_call(
        functools.partial(_peer_kernel, nblocks=nblocks, ntiles=ntiles, blocks_per_batch=bpb, lc=lc),
        grid=(nblocks * ntiles + 1,),
        in_specs=[row0, row0, row0, rown, rown, rown,
                  pl.BlockSpec((tb, d), lambda g: (jnp.minimum(g // ntiles, nblocks - 1), 0)),
                  pl.BlockSpec((d, te // 2), lambda g: (0, g % ntiles)),
                  pl.BlockSpec((d, te // 2), lambda g: (0, hb + g % ntiles)),
                  pl.BlockSpec((2, te // 2, d), lambda g: (0, (g + ntiles - 1) % ntiles, 0)),
                  pl.BlockSpec((tb, d), lambda g: (pblk(g), 0)),
                  pl.BlockSpec((1, 2, 8, d), lambda g: (pblk(g) // bpb, 0, 0, 0))],
        out_specs=pl.BlockSpec((tb, d), lambda g: (pblk(g), 0)),
        out_shape=jax.ShapeDtypeStruct((nt, d), F32),
        scratch_shapes=[pltpu.VMEM((2 * tb * PEER_PITCH, PEER_N_KEYS), jnp.uint32),
                        pltpu.VMEM((2, tb, te), BF16),
                        pltpu.VMEM((tb, d), F32)],
        compiler_params=_cparams("arbitrary"),
        name="peer",
    )(gate, i1, i2, gate, i1, i2, h2f, ut_bf, ut_bf, v3, xf, modp)


def _rope_tables(lc, l):
    t = jnp.arange(l)
    r = (t // GRID_W).astype(F32)
    col = (t % GRID_W).astype(F32)
    pairs = HEAD_DIM // 4
    inv = ROPE_BASE ** (-jnp.arange(pairs, dtype=F32) / pairs)
    ang_a = jnp.concatenate([r[:, None] * inv, col[:, None] * inv], axis=-1)
    half = RET_HEAD_DIM // 2
    inv_r = ROPE_BASE ** (-jnp.arange(half, dtype=F32) / half)
    ang_r = jnp.arange(l, dtype=F32)[:, None] * inv_r

    def tables(ang, heads):
        cos, sin = jnp.cos(ang), jnp.sin(ang)
        zero = jnp.zeros_like(sin)
        c = jnp.concatenate([cos, cos], -1)
        s1 = jnp.concatenate([-sin, zero], -1)
        s2 = jnp.concatenate([zero, sin], -1)
        w = c.shape[-1]
        ident = (jnp.ones((lc, w), F32), jnp.zeros((lc, w), F32), jnp.zeros((lc, w), F32))
        return tuple(jnp.tile(jnp.concatenate([i, x], 0), (1, heads)) for i, x in zip(ident, (c, s1, s2)))

    return tables(ang_a, N_HEADS) + tables(ang_r, RET_HEADS)


def kernel(x, c, ctx, c_ctx, ada_w, ada_b, norm1_w, norm2_w, w_in, w_out, q_norm_w, k_norm_w, attn_sink,
           s5_a_re, s5_a_im, s5_log_dt, s5_b_re, s5_b_im, s5_c_re, s5_c_im, s5_d, s5_glu_w, s5_glu_b,
           ret_decay, peer_wq, peer_keys, peer_u, peer_v):
    b, l, d = x.shape
    lc = ctx.shape[1]
    n = lc + l
    depth = ada_w.shape[0]
    tm = TOKEN_BLOCK
    ncb = lc // tm
    bpb = n // tm

    xa = jnp.concatenate([ctx, x], axis=1)
    rope_tabs = _rope_tables(lc, l)
    head_of = jnp.arange(ATT_Q_W) // HEAD_DIM
    bd = (head_of[:, None] == head_of[None, :]).astype(F32)
    crows = -(-(b + 1) // 8) * 8
    cpad = jnp.concatenate([c, c_ctx[None], jnp.zeros((crows - b - 1, d), F32)], axis=0)

    for layer in range(depth):
        mods = _adaln(cpad, ada_w[layer], ada_b[layer])
        mod = mods[:b].reshape(b, N_MOD, d)
        mod_c = jnp.broadcast_to(mods[b].reshape(1, N_MOD, d), (b, N_MOD, d))
        modp = jnp.pad(jnp.stack([mod_c, mod], axis=1), ((0, 0), (0, 0), (0, 8 - N_MOD), (0, 0)))

        z = _proj(xa, modp, norm1_w[layer], w_in[layer].astype(BF16), ncb)
        qnw = jnp.tile(q_norm_w[layer], N_HEADS).reshape(1, ATT_Q_W)
        knw = jnp.tile(k_norm_w[layer], N_KV_HEADS).reshape(1, ATT_KV_W)
        qa, ka, va, su, rq, rk, rv, rg = _prep(z, rope_tabs, qnw, knw, bd)

        att = _attention(attn_sink[layer].astype(F32), qa, ka, va, lc)
        mats = _s5_mats(s5_a_re[layer], s5_a_im[layer], s5_log_dt[layer], s5_b_re[layer], s5_b_im[layer],
                        s5_c_re[layer], s5_c_im[layer])
        ssm = _s5_group(su, mats, s5_d[layer], s5_glu_w[layer].astype(BF16), s5_glu_b[layer], lc)
        ret = _retention(rq, rk, rv, rg, _ret_tables(ret_decay[layer]), lc)

        xm, h2 = _mix(xa, att, ssm, ret, w_out[layer].astype(BF16), modp, norm2_w[layer], ncb)

        h2f = h2.reshape(b * n, d)
        keys = peer_keys[layer].reshape(2 * PEER_HEADS, PEER_N_KEYS, PEER_HALF).astype(BF16)
        gate, i1, i2 = _peer_select(h2f, peer_wq[layer].T.astype(BF16), keys)
        xf = _peer(gate, i1, i2, h2f, peer_u[layer].T.astype(BF16), peer_v[layer].astype(BF16),
                   xm.reshape(b * n, d), modp, n, lc)
        xa = xf.reshape(b, n, d)

    return xa[:, lc:]
```

```python
import functools
import math

import jax
import jax.numpy as jnp
from jax import lax
from jax.experimental import pallas as pl
from jax.experimental.pallas import tpu as pltpu

F32 = jnp.float32
BF16 = jnp.bfloat16
I32 = jnp.int32
HI = lax.Precision.HIGHEST

D_MODEL = 1024
N_MOD = 6
EPS = 1e-6
GRID_W = 64

N_HEADS = 8
N_KV_HEADS = 2
GQA = N_HEADS // N_KV_HEADS
HEAD_DIM = 64
ATT_BLOCK = 128
ROPE_BASE = 10000.0
ATT_Q_W = N_HEADS * HEAD_DIM
ATT_KV_W = N_KV_HEADS * HEAD_DIM

S5_GROUPS = 16
S5_GROUP_CH = 16
S5_STATE = 64
S5_W = S5_GROUPS * S5_GROUP_CH
S5_CHUNK = 4
S5_BLOCK_CHUNKS = 32

RET_HEADS = 4
RET_HEAD_DIM = 64
RET_W = RET_HEADS * RET_HEAD_DIM
RET_CHUNK = 128

MIX_W = ATT_Q_W + S5_W + RET_W
PROJ_W = ATT_Q_W + 2 * ATT_KV_W + S5_W + 4 * RET_W

PEER_HEADS = 8
PEER_N_KEYS = 128
PEER_EXPERTS = PEER_N_KEYS * PEER_N_KEYS
PEER_QUERY_DIM = 128
PEER_HALF = PEER_QUERY_DIM // 2
PEER_TOPK = 16

TOKEN_BLOCK = 256
PEER_TOKEN_BLOCK = 384
PEER_EXPERT_TILE = 1024
PEER_PITCH = PEER_N_KEYS // 2 + 8
VMEM_LIMIT = 56 * 1024 * 1024


def _cparams(*sem):
    return pltpu.CompilerParams(dimension_semantics=sem, vmem_limit_bytes=VMEM_LIMIT)


def _adaln_kernel(c_ref, w_ref, b_ref, o_ref):
    c = c_ref[...]
    s = c * jax.nn.sigmoid(c)
    o_ref[...] = jnp.dot(s, w_ref[...], precision=HI, preferred_element_type=F32) + b_ref[...]


def _adaln(cpad, w, b):
    rows, d = cpad.shape
    n = w.shape[1]
    tn = 1536
    return pl.pallas_call(
        _adaln_kernel,
        grid=(n // tn,),
        in_specs=[pl.BlockSpec((rows, d), lambda j: (0, 0)),
                  pl.BlockSpec((d, tn), lambda j: (0, j)),
                  pl.BlockSpec((1, tn), lambda j: (0, j))],
        out_specs=pl.BlockSpec((rows, tn), lambda j: (0, j)),
        out_shape=jax.ShapeDtypeStruct((rows, n), F32),
        compiler_params=_cparams("arbitrary"),
        name="adaln",
    )(cpad, w, b.reshape(1, n))


def _proj_kernel(x_ref, mod_ref, nw_ref, w_ref, z_ref):
    x = x_ref[0]
    ms = jnp.mean(x * x, axis=-1, keepdims=True)
    y = x * lax.rsqrt(ms + EPS) * nw_ref[...]
    m = mod_ref[0, 0]
    h = y * (1.0 + m[1:2]) + m[0:1]
    z_ref[0] = jnp.dot(h.astype(BF16), w_ref[...], preferred_element_type=F32)


def _proj(xa, modp, nw, w_bf, ncb):
    b, n, d = xa.shape
    tm = TOKEN_BLOCK
    pw = w_bf.shape[1]
    return pl.pallas_call(
        _proj_kernel,
        grid=(b, n // tm),
        in_specs=[pl.BlockSpec((1, tm, d), lambda i, j: (i, j, 0)),
                  pl.BlockSpec((1, 1, 8, d), lambda i, j: (i, jnp.minimum(j // ncb, 1), 0, 0)),
                  pl.BlockSpec((1, d), lambda i, j: (0, 0)),
                  pl.BlockSpec((d, pw), lambda i, j: (0, 0))],
        out_specs=pl.BlockSpec((1, tm, pw), lambda i, j: (i, j, 0)),
        out_shape=jax.ShapeDtypeStruct((b, n, pw), F32),
        compiler_params=_cparams("arbitrary", "arbitrary"),
        name="proj",
    )(xa, modp, nw.reshape(1, d), w_bf)


def _rope(x, c, s1, s2):
    w = x.shape[-1]
    return x * c + pltpu.roll(x, w - 32, 1) * s1 + pltpu.roll(x, 32, 1) * s2


def _prep_kernel(z_ref, cq_ref, s1q_ref, s2q_ref, cr_ref, s1r_ref, s2r_ref, qnw_ref, knw_ref, bd_ref,
                 qa_ref, ka_ref, va_ref, su_ref, rq_ref, rk_ref, rv_ref, rg_ref, su2_ref, su_sc):
    o = 0
    aq = z_ref[0, :, o:o + ATT_Q_W]; o += ATT_Q_W
    ak = z_ref[0, :, o:o + ATT_KV_W]; o += ATT_KV_W
    av = z_ref[0, :, o:o + ATT_KV_W]; o += ATT_KV_W
    su = z_ref[0, :, o:o + S5_W]; o += S5_W
    rq = z_ref[0, :, o:o + RET_W]; o += RET_W
    rk = z_ref[0, :, o:o + RET_W]; o += RET_W
    rv = z_ref[0, :, o:o + RET_W]; o += RET_W
    rg = z_ref[0, :, o:o + RET_W]

    bd = bd_ref[...]
    inv = 1.0 / HEAD_DIM
    msq = jnp.dot(aq * aq, bd, precision=HI, preferred_element_type=F32) * inv
    qn = aq * lax.rsqrt(msq + EPS) * qnw_ref[...]
    msk = jnp.dot(ak * ak, bd[:ATT_KV_W, :ATT_KV_W], precision=HI, preferred_element_type=F32) * inv
    kn = ak * lax.rsqrt(msk + EPS) * knw_ref[...]

    cq, s1q, s2q = cq_ref[...], s1q_ref[...], s2q_ref[...]
    qa_ref[0] = (_rope(qn, cq, s1q, s2q) * (HEAD_DIM ** -0.5)).astype(BF16)
    ka_ref[0] = _rope(kn, cq[:, :ATT_KV_W], s1q[:, :ATT_KV_W], s2q[:, :ATT_KV_W]).astype(BF16)
    va_ref[0] = av.astype(BF16)
    su_ref[0] = su
    for hf in range(S5_W // 128):
        su_sc[hf] = su[:, hf * 128:(hf + 1) * 128]
    for tt in range(S5_CHUNK):
        for hf in range(S5_W // 128):
            c0 = tt * S5_W + hf * 128
            su2_ref[0, :, c0:c0 + 128] = su_sc[hf, pl.ds(tt, su2_ref.shape[1], stride=S5_CHUNK), :]
    cr, s1r, s2r = cr_ref[...], s1r_ref[...], s2r_ref[...]
    rq_ref[0] = (_rope(rq, cr, s1r, s2r) * (RET_HEAD_DIM ** -0.5)).astype(BF16)
    rk_ref[0] = _rope(rk, cr, s1r, s2r).astype(BF16)
    rv_ref[0] = rv.astype(BF16)
    rg_ref[0] = rg


def _prep(z, tabs, qnw, knw, bd):
    b, n, pw = z.shape
    tm = TOKEN_BLOCK
    cq, s1q, s2q, cr, s1r, s2r = tabs

    def tok(w):
        return pl.BlockSpec((1, tm, w), lambda j, i: (i, j, 0))

    def tab(w):
        return pl.BlockSpec((tm, w), lambda j, i: (j, 0))

    def full(shape):
        return pl.BlockSpec(shape, lambda j, i: (0,) * len(shape))

    widths = (ATT_Q_W, ATT_KV_W, ATT_KV_W, S5_W, RET_W, RET_W, RET_W, RET_W)
    dtypes = (BF16, BF16, BF16, F32, BF16, BF16, BF16, F32)
    return pl.pallas_call(
        _prep_kernel,
        grid=(n // tm, b),
        in_specs=[tok(pw), tab(ATT_Q_W), tab(ATT_Q_W), tab(ATT_Q_W), tab(RET_W), tab(RET_W), tab(RET_W),
                  full((1, ATT_Q_W)), full((1, ATT_KV_W)), full((ATT_Q_W, ATT_Q_W))],
        out_specs=[tok(w) for w in widths] + [pl.BlockSpec((1, tm // S5_CHUNK, S5_CHUNK * S5_W),
                                                           lambda j, i: (i, j, 0))],
        out_shape=[jax.ShapeDtypeStruct((b, n, w), dt) for w, dt in zip(widths, dtypes)]
        + [jax.ShapeDtypeStruct((b, n // S5_CHUNK, S5_CHUNK * S5_W), F32)],
        scratch_shapes=[pltpu.VMEM((S5_W // 128, tm, 128), F32)],
        compiler_params=_cparams("arbitrary", "arbitrary"),
        name="prep",
    )(z, cq, s1q, s2q, cr, s1r, s2r, qnw, knw, bd)


def _attn_kernel(sink_ref, q_ref, k_ref, v_ref, o_ref, *, lc, l):
    blk = ATT_BLOCK
    nb = l // blk
    n = pl.program_id(1) - lc // blk
    ctx_off = jnp.where(n >= 0, 0, -(1 << 20))

    def kstart(m):
        return pl.multiple_of(lc + jnp.clip(m, 0, nb - 1) * blk, blk)

    starts = (kstart(n - 1), kstart(n), kstart(n + 1))
    rows = GQA * blk
    row = lax.broadcasted_iota(I32, (rows, 3 * blk), 0)
    col = lax.broadcasted_iota(I32, (rows, 3 * blk), 1)
    qi = row & (blk - 1)
    dist = col - blk - qi
    kpos = (n - 1) * blk + ctx_off + col
    valid = (jnp.abs(dist) <= blk) & (kpos >= 0) & (kpos < l)
    grow = lax.broadcasted_iota(I32, (rows, 1), 0) // blk

    for hkv in range(N_KV_HEADS):
        q4 = jnp.concatenate(
            [q_ref[0, :, (hkv * GQA + g) * HEAD_DIM:(hkv * GQA + g + 1) * HEAD_DIM] for g in range(GQA)], axis=0)
        ksl = slice(hkv * HEAD_DIM, (hkv + 1) * HEAD_DIM)
        kcat = jnp.concatenate([k_ref[0, pl.ds(s, blk), ksl] for s in starts] + [k_ref[0, 0:lc, ksl]], axis=0)
        vcat = jnp.concatenate([v_ref[0, pl.ds(s, blk), ksl] for s in starts] + [v_ref[0, 0:lc, ksl]], axis=0)
        s = lax.dot_general(q4, kcat, (((1,), (1,)), ((), ())), preferred_element_type=F32)
        s_win = jnp.where(valid, s[:, :3 * blk], -jnp.inf)
        s_ctx = s[:, 3 * blk:]
        sink = jnp.zeros((rows, 1), F32)
        for g in range(GQA):
            sink = jnp.where(grow == g, sink_ref[hkv * GQA + g], sink)
        m = jnp.maximum(jnp.maximum(jnp.max(s_win, axis=-1, keepdims=True),
                                    jnp.max(s_ctx, axis=-1, keepdims=True)), sink)
        e_win = jnp.exp(s_win - m)
        e_ctx = jnp.exp(s_ctx - m)
        den = (jnp.sum(e_win, axis=-1, keepdims=True) + jnp.sum(e_ctx, axis=-1, keepdims=True)
               + jnp.exp(sink - m))
        p = jnp.concatenate([e_win, e_ctx], axis=1).astype(BF16)
        o = jnp.dot(p, vcat, preferred_element_type=F32) / den
        for g in range(GQA):
            c0 = (hkv * GQA + g) * HEAD_DIM
            o_ref[0, :, c0:c0 + HEAD_DIM] = o[g * blk:(g + 1) * blk].astype(o_ref.dtype)


def _attention(sink, qa, ka, va, lc):
    b, n, _ = qa.shape
    return pl.pallas_call(
        functools.partial(_attn_kernel, lc=lc, l=n - lc),
        grid=(b, n // ATT_BLOCK),
        in_specs=[pl.BlockSpec(memory_space=pltpu.SMEM),
                  pl.BlockSpec((1, ATT_BLOCK, ATT_Q_W), lambda i, j: (i, j, 0)),
                  pl.BlockSpec((1, n, ATT_KV_W), lambda i, j: (i, 0, 0)),
                  pl.BlockSpec((1, n, ATT_KV_W), lambda i, j: (i, 0, 0))],
        out_specs=pl.BlockSpec((1, ATT_BLOCK, ATT_Q_W), lambda i, j: (i, j, 0)),
        out_shape=jax.ShapeDtypeStruct((b, n, ATT_Q_W), BF16),
        compiler_params=_cparams("arbitrary", "arbitrary"),
        name="attention",
    )(sink, qa, ka, va)


def _s5_increments(u_ref, w_ref, inc_sc, nb, cpb):
    k = u_ref.shape[-1]
    u = u_ref[...].reshape(nb * cpb, k).astype(BF16)
    inc = jnp.dot(u, w_ref[...], preferred_element_type=F32)
    pitch = cpb + 8
    for b in range(nb):
        for s in range(inc.shape[-1] // 128):
            inc_sc[s, b * pitch:b * pitch + cpb, :] = inc[b * cpb:(b + 1) * cpb, s * 128:(s + 1) * 128]
    return u


def _s5_sweep(a_ref, inc_sc, hin_sc, h_sc, nb, cpb, reverse):
    pitch = cpb + 8
    sw = h_sc.shape[-1]
    nsl = sw // 128
    are = a_ref[0:1]
    aim = a_ref[1:2]

    def step(i, h):
        c = cpb - 1 - i if reverse else i
        rows = pl.ds(c, nb, stride=pitch)
        for s in range(nsl):
            hin_sc[s, rows, :] = h[:, s * 128:(s + 1) * 128]
        inc = jnp.concatenate([inc_sc[s, rows, :] for s in range(nsl)], axis=1)
        swapped = jnp.concatenate([h[:, sw // 2:], h[:, :sw // 2]], axis=1)
        return h * are + swapped * aim + inc

    h_sc[...] = lax.fori_loop(0, cpb, step, h_sc[...])


def _s5_states(hin_sc, nb, cpb):
    pitch = cpb + 8
    nsl = hin_sc.shape[0]
    return jnp.concatenate(
        [jnp.concatenate([hin_sc[s, b * pitch:b * pitch + cpb, :] for s in range(nsl)], axis=1) for b in range(nb)],
        axis=0)


def _s5_fwd_kernel(u_ref, w_ref, a_ref, hf_ref, inc_sc, hin_sc, h_sc):
    nb, cpb, _ = u_ref.shape

    @pl.when(pl.program_id(0) == 0)
    def _():
        h_sc[...] = jnp.zeros_like(h_sc)

    _s5_increments(u_ref, w_ref, inc_sc, nb, cpb)
    _s5_sweep(a_ref, inc_sc, hin_sc, h_sc, nb, cpb, False)
    hf_ref[...] = _s5_states(hin_sc, nb, cpb).astype(hf_ref.dtype).reshape(hf_ref.shape)


def _s5_bwd_kernel(u_ref, hf_ref, w_ref, m_ref, vf_ref, vb_ref, a_ref, y_ref, inc_sc, hin_sc, h_sc):
    nb, cpb, _ = u_ref.shape

    @pl.when(pl.program_id(0) == 0)
    def _():
        h_sc[...] = jnp.zeros_like(h_sc)

    u = _s5_increments(u_ref, w_ref, inc_sc, nb, cpb)
    _s5_sweep(a_ref, inc_sc, hin_sc, h_sc, nb, cpb, True)
    hb = _s5_states(hin_sc, nb, cpb).astype(BF16)
    hf = hf_ref[...].reshape(nb * cpb, hf_ref.shape[-1])
    y = (jnp.dot(u, m_ref[...], preferred_element_type=F32)
         + jnp.dot(hf, vf_ref[...], preferred_element_type=F32)
         + jnp.dot(hb, vb_ref[...], preferred_element_type=F32))
    y_ref[...] = y.reshape(y_ref.shape)


def _s5_scan(u2, mats, lc):
    b, nc, k = u2.shape
    m, wf, wb, vf, vb, af, ab = mats
    sw = wf.shape[1]
    cpb = S5_BLOCK_CHUNKS
    nblk = nc // cpb
    cblk = lc // (cpb * S5_CHUNK)
    pitch = cpb + 8
    scratch = [pltpu.VMEM((sw // 128, b * pitch, 128), F32), pltpu.VMEM((sw // 128, b * pitch, 128), F32),
               pltpu.VMEM((b, sw), F32)]

    def full(arr):
        return pl.BlockSpec(arr.shape, lambda j: (0,) * arr.ndim)

    hf = pl.pallas_call(
        _s5_fwd_kernel,
        grid=(nblk,),
        in_specs=[pl.BlockSpec((b, cpb, k), lambda j: (0, j, 0)), full(wf), full(af)],
        out_specs=pl.BlockSpec((b, cpb, sw), lambda j: (0, j, 0)),
        out_shape=jax.ShapeDtypeStruct((b, nc, sw), BF16),
        scratch_shapes=scratch,
        compiler_params=_cparams("arbitrary"),
        name="s5_fwd",
    )(u2, wf, af)

    def order(j):
        return jnp.where(j < cblk, cblk - 1 - j, nblk - 1 - (j - cblk))

    return pl.pallas_call(
        _s5_bwd_kernel,
        grid=(nblk,),
        in_specs=[pl.BlockSpec((b, cpb, k), lambda j: (0, order(j), 0)),
                  pl.BlockSpec((b, cpb, sw), lambda j: (0, order(j), 0)),
                  full(wb), full(m), full(vf), full(vb), full(ab)],
        out_specs=pl.BlockSpec((b, cpb, k), lambda j: (0, order(j), 0)),
        out_shape=jax.ShapeDtypeStruct((b, nc, k), F32),
        scratch_shapes=scratch,
        compiler_params=_cparams("arbitrary"),
        name="s5_bwd",
    )(u2, hf, wb, m, vf, vb, ab)


def _s5_out_kernel(y_ref, u_ref, d_ref, w_ref, b_ref, o_ref, sc):
    t = S5_CHUNK
    rows = y_ref.shape[1]
    for tt in range(t):
        for hf in range(S5_W // 128):
            c0 = tt * S5_W + hf * 128
            sc[hf, pl.ds(tt, rows, stride=t), :] = y_ref[0, :, c0:c0 + 128]
    yt = jnp.concatenate([sc[hf] for hf in range(S5_W // 128)], axis=1)
    y = jax.nn.gelu(yt + d_ref[...] * u_ref[0])
    gate = jax.nn.sigmoid(jnp.dot(y.astype(BF16), w_ref[...], preferred_element_type=F32) + b_ref[...])
    o_ref[0] = (y * gate).astype(o_ref.dtype)


def _s5_out(y2, u, d, w_bf, bvec):
    b, n, w = u.shape
    tm = TOKEN_BLOCK
    tok = pl.BlockSpec((1, tm, w), lambda i, j: (i, j, 0))
    vec = pl.BlockSpec((1, w), lambda i, j: (0, 0))
    return pl.pallas_call(
        _s5_out_kernel,
        grid=(b, n // tm),
        in_specs=[pl.BlockSpec((1, tm // S5_CHUNK, S5_CHUNK * w), lambda i, j: (i, j, 0)), tok, vec,
                  pl.BlockSpec((w, w), lambda i, j: (0, 0)), vec],
        out_specs=tok,
        out_shape=jax.ShapeDtypeStruct((b, n, w), BF16),
        scratch_shapes=[pltpu.VMEM((w // 128, tm, 128), F32)],
        compiler_params=_cparams("arbitrary", "arbitrary"),
        name="s5_out",
    )(y2, u, d.reshape(1, w), w_bf, bvec.reshape(1, w))


def _s5_mats(a_re, a_im, log_dt, b_re, b_im, c_re, c_im):
    t = S5_CHUNK
    g, p, h = S5_GROUPS, S5_STATE, S5_GROUP_CH
    a_re, a_im, b_re, b_im, c_re, c_im = (v.astype(F32) for v in (a_re, a_im, b_re, b_im, c_re, c_im))
    dt = jnp.exp(log_dt.astype(F32))
    steps = jnp.arange(t + 1, dtype=F32)[None, :, None, None]
    mag = jnp.exp((a_re * dt)[:, None] * steps)
    ang = (a_im * dt)[:, None] * steps
    pr, pi = mag * jnp.cos(ang), mag * jnp.sin(ang)
    nr, ni = pr[:, 1] - 1.0, pi[:, 1]
    den = a_re * a_re + a_im * a_im
    qr, qi = (nr * a_re + ni * a_im) / den, (ni * a_re - nr * a_im) / den
    bbr = qr[..., None] * b_re - qi[..., None] * b_im
    bbi = qr[..., None] * b_im + qi[..., None] * b_re
    wr = pr[..., None] * bbr[:, None] - pi[..., None] * bbi[:, None]
    wi = pr[..., None] * bbi[:, None] + pi[..., None] * bbr[:, None]
    kern = (jnp.einsum('dgip,dkgpj->dkgij', c_re, wr, precision=HI)
            - jnp.einsum('dgip,dkgpj->dkgij', c_im, wi, precision=HI))
    eye = jnp.eye(g, dtype=F32)

    s_idx = jnp.arange(t)[:, None]
    t_idx = jnp.arange(t)[None, :]
    dist = t_idx - s_idx
    mf = jnp.where((dist >= 0)[:, :, None, None, None], kern[0][jnp.clip(dist, 0, t)], 0.0)
    mb = jnp.where((dist <= 0)[:, :, None, None, None], kern[1][jnp.clip(-dist, 0, t)], 0.0)
    m = jnp.einsum('stgij,gq->sgjtqi', mf + mb, eye).reshape(t * g * h, t * g * h)

    def inc_op(d, ks):
        parts = [jnp.einsum('sgpj,gq->sgjqp', w[d][ks], eye) for w in (wr, wi)]
        return jnp.stack(parts, axis=3).reshape(t * g * h, 2 * g * p)

    def read_op(d, ks):
        ar, ai = pr[d][ks][:, :, None, :], pi[d][ks][:, :, None, :]
        zr = c_re[d][None] * ar - c_im[d][None] * ai
        zi = c_re[d][None] * ai + c_im[d][None] * ar
        parts = [jnp.einsum('tgip,gq->qptgi', z, eye) for z in (zr, -zi)]
        return jnp.stack(parts, axis=0).reshape(2 * g * p, t * g * h)

    def decay(d):
        re, im = pr[d, t].reshape(-1), pi[d, t].reshape(-1)
        return jnp.stack([jnp.concatenate([re, re]), jnp.concatenate([-im, im])], axis=0)

    ar = jnp.arange(t)
    bf16 = lambda x: x.astype(BF16)
    return (bf16(m), bf16(inc_op(0, t - 1 - ar)), bf16(inc_op(1, ar)), bf16(read_op(0, 1 + ar)),
            bf16(read_op(1, t - ar)), decay(0), decay(1))


def _s5_group(su, su2, mats, d, glu_w_bf, glu_b, lc):
    y2 = _s5_scan(su2, mats, lc)
    return _s5_out(y2, su, d, glu_w_bf, glu_b)


def _ret_kernel(cd_ref, q_ref, k_ref, v_ref, g_ref, dm_ref, qwf_ref, kwf_ref, qwb_ref, kwb_ref, o_ref,
                sf_store, sf, sb, *, nchunk, nchunk_ctx):
    ck = RET_CHUNK
    hd = RET_HEAD_DIM
    tdims = (((0,), (0,)), ((), ()))

    def rows(c):
        return pl.ds(pl.multiple_of(c * ck, ck), ck)

    sf[...] = jnp.zeros_like(sf)
    sb[...] = jnp.zeros_like(sb)

    def fwd(c, carry):
        kc = k_ref[0, rows(c), :].astype(F32) * kwf_ref[...]
        vc = v_ref[0, rows(c), :]
        for h in range(RET_HEADS):
            hs = slice(h * hd, (h + 1) * hd)
            sf_store[c, h] = sf[h]
            kv = lax.dot_general(kc[:, hs].astype(BF16), vc[:, hs], tdims, preferred_element_type=F32)
            sf[h] = cd_ref[0, h] * sf[h] + kv
        return carry

    lax.fori_loop(0, nchunk, fwd, 0)

    def bwd(c, carry):
        qc = q_ref[0, rows(c), :].astype(F32)
        kc_b = k_ref[0, rows(c), :]
        vc = v_ref[0, rows(c), :]
        gc = g_ref[0, rows(c), :]
        qf = (qc * qwf_ref[...]).astype(BF16)
        qb = (qc * qwb_ref[...]).astype(BF16)
        kb = (kc_b.astype(F32) * kwb_ref[...]).astype(BF16)
        q16 = q_ref[0, rows(c), :]
        for h in range(RET_HEADS):
            hs = slice(h * hd, (h + 1) * hd)
            sc = lax.dot_general(q16[:, hs], kc_b[:, hs], (((1,), (1,)), ((), ())), preferred_element_type=F32)
            sc = (sc * dm_ref[h]).astype(BF16)
            o = jnp.dot(sc, vc[:, hs], preferred_element_type=F32)
            o = o + jnp.dot(qf[:, hs], sf_store[c, h].astype(BF16), preferred_element_type=F32)
            o = o + jnp.dot(qb[:, hs], sb[h].astype(BF16), preferred_element_type=F32)
            o = o * lax.rsqrt(jnp.mean(o * o, axis=-1, keepdims=True) + EPS)
            gh = gc[:, hs]
            o_ref[0, rows(c), hs] = (o * (gh * jax.nn.sigmoid(gh))).astype(o_ref.dtype)
            kv = lax.dot_general(kb[:, hs], vc[:, hs], tdims, preferred_element_type=F32)
            sb[h] = cd_ref[1, h] * sb[h] + kv
        return carry

    lax.fori_loop(0, nchunk_ctx, lambda i, cr: bwd(nchunk_ctx - 1 - i, cr), 0)
    lax.fori_loop(0, nchunk - nchunk_ctx, lambda i, cr: bwd(nchunk - 1 - i, cr), 0)


def _retention(rq, rk, rv, rg, tabs, lc):
    b, n, w = rq.shape
    cd, dm, qwf, kwf, qwb, kwb = tabs
    nchunk = n // RET_CHUNK
    tok = pl.BlockSpec((1, n, w), lambda i: (i, 0, 0))
    tab = pl.BlockSpec((RET_CHUNK, w), lambda i: (0, 0))
    st = pltpu.VMEM((RET_HEADS, RET_HEAD_DIM, RET_HEAD_DIM), F32)
    return pl.pallas_call(
        functools.partial(_ret_kernel, nchunk=nchunk, nchunk_ctx=lc // RET_CHUNK),
        grid=(b,),
        in_specs=[pl.BlockSpec(memory_space=pltpu.SMEM), tok, tok, tok, tok,
                  pl.BlockSpec((RET_HEADS, RET_CHUNK, RET_CHUNK), lambda i: (0, 0, 0)), tab, tab, tab, tab],
        out_specs=tok,
        out_shape=jax.ShapeDtypeStruct((b, n, w), BF16),
        scratch_shapes=[pltpu.VMEM((nchunk, RET_HEADS, RET_HEAD_DIM, RET_HEAD_DIM), F32), st, st],
        compiler_params=_cparams("arbitrary"),
        name="retention",
    )(cd, rq, rk, rv, rg, dm, qwf, kwf, qwb, kwb)


def _ret_tables(decay_logit):
    lg = jax.nn.log_sigmoid(decay_logit.astype(F32))
    pos = jnp.arange(RET_CHUNK, dtype=F32)
    rel = pos[:, None] - pos[None, :]
    dm = jnp.where(rel > 0, jnp.exp(lg[0][:, None, None] * jnp.maximum(rel, 0.0)),
                   jnp.where(rel < 0, jnp.exp(lg[1][:, None, None] * jnp.maximum(-rel, 0.0)), 2.0))

    def wide(e):
        return jnp.repeat(e.T, RET_HEAD_DIM, axis=1)

    qwf = wide(jnp.exp(lg[0][:, None] * (pos + 1.0)))
    kwf = wide(jnp.exp(lg[0][:, None] * (RET_CHUNK - 1.0 - pos)))
    qwb = wide(jnp.exp(lg[1][:, None] * (RET_CHUNK - pos)))
    kwb = wide(jnp.exp(lg[1][:, None] * pos))
    cd = jnp.exp(lg * RET_CHUNK)
    return cd, dm, qwf, kwf, qwb, kwb


def _mix_kernel(x_ref, att_ref, ssm_ref, ret_ref, w_ref, mod_ref, nw_ref, xo_ref, h2_ref):
    m = mod_ref[0, 0]
    y = jnp.dot(att_ref[0], w_ref[0:ATT_Q_W, :], preferred_element_type=F32)
    y = y + jnp.dot(ssm_ref[0], w_ref[ATT_Q_W:ATT_Q_W + S5_W, :], preferred_element_type=F32)
    y = y + jnp.dot(ret_ref[0], w_ref[ATT_Q_W + S5_W:MIX_W, :], preferred_element_type=F32)
    x = x_ref[0] + m[2:3] * y
    xo_ref[0] = x
    ms = jnp.mean(x * x, axis=-1, keepdims=True)
    h = x * lax.rsqrt(ms + EPS) * nw_ref[...]
    h2_ref[0] = (h * (1.0 + m[4:5]) + m[3:4]).astype(BF16)


def _mix(xa, att, ssm, ret, w_bf, modp, nw, ncb):
    b, n, d = xa.shape
    tm = TOKEN_BLOCK

    def tok(w):
        return pl.BlockSpec((1, tm, w), lambda i, j: (i, j, 0))

    return pl.pallas_call(
        _mix_kernel,
        grid=(b, n // tm),
        in_specs=[tok(d), tok(ATT_Q_W), tok(S5_W), tok(RET_W),
                  pl.BlockSpec((MIX_W, d), lambda i, j: (0, 0)),
                  pl.BlockSpec((1, 1, 8, d), lambda i, j: (i, jnp.minimum(j // ncb, 1), 0, 0)),
                  pl.BlockSpec((1, d), lambda i, j: (0, 0))],
        out_specs=[tok(d), tok(d)],
        out_shape=[jax.ShapeDtypeStruct((b, n, d), F32), jax.ShapeDtypeStruct((b, n, d), BF16)],
        compiler_params=_cparams("arbitrary", "arbitrary"),
        name="mix",
    )(xa, att, ssm, ret, w_bf, modp, nw.reshape(1, d))


def _peer_select_kernel_rows(h_ref, wqt_ref, keys_ref, gate_ref, i1_ref, i2_ref,
                             s_sc, t_sc, i_sc, best_sc, sel_sc, g_all, e1_all, e2_all):
    tb = h_ref.shape[0]
    kk = PEER_TOPK
    nk = PEER_N_KEYS
    ninf = -jnp.inf
    qt = lax.dot_general(wqt_ref[...], h_ref[...], (((1,), (1,)), ((), ())), preferred_element_type=F32)
    qt = qt.astype(BF16)
    for hh in range(2 * PEER_HEADS):
        h, half = divmod(hh, 2)
        r0 = h * PEER_QUERY_DIM + half * PEER_HALF
        s_sc[hh] = jnp.dot(keys_ref[half * PEER_HEADS + h], qt[r0:r0 + PEER_HALF],
                           preferred_element_type=F32)

    kiota = lax.broadcasted_iota(I32, (nk, tb), 0)

    def stage1(hh, carry):
        s = s_sc[hh]
        for it in range(kk):
            m = jnp.max(s, axis=0, keepdims=True)
            idx = jnp.min(jnp.where(s == m, kiota, nk), axis=0, keepdims=True)
            t_sc[hh, it:it + 1, :] = m
            i_sc[hh, it:it + 1, :] = idx
            s = jnp.where(kiota == idx, ninf, s)
        return carry

    lax.fori_loop(0, 2 * PEER_HEADS, stage1, 0)

    ncand = 80
    rho = lax.broadcasted_iota(I32, (ncand, tb), 0)
    tile = rho >> 3
    wi = rho & 7
    ca = jnp.where(tile <= 1, 0, jnp.where(tile == 9, 8 + wi, tile - 1))
    cb = jnp.where(tile == 1, 8 + wi, jnp.where(tile == 9, 0, wi))
    flat = ca * kk + cb
    cvalid = (ca + 1) * (cb + 1) <= kk

    def stage2(h, carry):
        t1 = t_sc[2 * h]
        t2 = t_sc[2 * h + 1]
        tiles = [t1[0:1] + t2[0:8], t1[0:1] + t2[8:16]]
        tiles += [t1[a:a + 1] + t2[0:8] for a in range(1, 8)]
        tiles += [t1[8:16] + t2[0:1]]
        cand = jnp.where(cvalid, jnp.concatenate(tiles, axis=0), ninf)
        for it in range(kk):
            m = jnp.max(cand, axis=0, keepdims=True)
            idx = jnp.min(jnp.where(cand == m, flat, kk * kk), axis=0, keepdims=True)
            best_sc[it:it + 1, :] = m
            sel_sc[it:it + 1, :] = idx
            cand = jnp.where(flat == idx, ninf, cand)
        best = best_sc[...]
        sel = sel_sc[...]
        asel = sel >> 4
        bsel = sel & (kk - 1)
        i1 = i_sc[2 * h]
        i2 = i_sc[2 * h + 1]
        e1 = jnp.zeros((kk, tb), I32)
        e2 = jnp.zeros((kk, tb), I32)
        for a in range(kk):
            e1 = jnp.where(asel == a, i1[a:a + 1], e1)
            e2 = jnp.where(bsel == a, i2[a:a + 1], e2)
        ex = jnp.exp(best - best[0:1])
        gate = ex / jnp.sum(ex, axis=0, keepdims=True)
        r = pl.ds(pl.multiple_of(h * kk, kk), kk)
        g_all[r, :] = gate
        e1_all[r, :] = e1
        e2_all[r, :] = e2
        return carry

    lax.fori_loop(0, PEER_HEADS, stage2, 0)
    gate_ref[...] = g_all[...].T
    i1_ref[...] = e1_all[...].T
    i2_ref[...] = e2_all[...].T


def _peer_select(h2f, wqt_bf, keys_bf):
    nt, d = h2f.shape
    tb = TOKEN_BLOCK
    r = PEER_HEADS * PEER_TOPK
    row = pl.BlockSpec((tb, r), lambda i: (i, 0))
    return pl.pallas_call(
        _peer_select_kernel,
        grid=(nt // tb,),
        in_specs=[pl.BlockSpec((tb, d), lambda i: (i, 0)),
                  pl.BlockSpec((d, d), lambda i: (0, 0)),
                  pl.BlockSpec((2 * PEER_HEADS, PEER_N_KEYS, PEER_HALF), lambda i: (0, 0, 0))],
        out_specs=[row, row, row],
        out_shape=[jax.ShapeDtypeStruct((nt, r), F32), jax.ShapeDtypeStruct((nt, r), I32),
                   jax.ShapeDtypeStruct((nt, r), I32)],
        scratch_shapes=[pltpu.VMEM((2 * PEER_HEADS, PEER_N_KEYS, tb), F32),
                        pltpu.VMEM((2 * PEER_HEADS, PEER_TOPK, tb), F32),
                        pltpu.VMEM((2 * PEER_HEADS, PEER_TOPK, tb), I32),
                        pltpu.VMEM((PEER_TOPK, tb), F32),
                        pltpu.VMEM((PEER_TOPK, tb), I32),
                        pltpu.VMEM((r, tb), F32),
                        pltpu.VMEM((r, tb), I32),
                        pltpu.VMEM((r, tb), I32)],
        compiler_params=_cparams("arbitrary"),
        name="peer_select",
    )(h2f, wqt_bf, keys_bf)


SEL_PITCH = PEER_N_KEYS + 8


def _bitonic_stages(n):
    stages = []
    k = 2
    while k <= n:
        j = k // 2
        while j >= 1:
            stages.append([(i, i ^ j, (i & k) == 0) for i in range(n) if (i ^ j) > i])
            j //= 2
        k *= 2
    return stages


def _ranks_before(ka, ia, kb, ib):
    eq = ka == kb
    return jnp.where(eq, ia, kb) < jnp.where(eq, ib, ka)


def _exchange(ks, ix, stages):
    ks, ix = list(ks), list(ix)
    for stage in stages:
        for i, j, first_at_i in stage:
            c = _ranks_before(ks[i], ix[i], ks[j], ix[j])
            lo, hi = (i, j) if first_at_i else (j, i)
            ks[lo], ks[hi] = jnp.where(c, ks[i], ks[j]), jnp.where(c, ks[j], ks[i])
            ix[lo], ix[hi] = jnp.where(c, ix[i], ix[j]), jnp.where(c, ix[j], ix[i])
    return ks, ix


def _order_key(v):
    b = lax.bitcast_convert_type(v, I32)
    return b ^ ((b >> 31) & jnp.int32(0x7FFFFFFF))


def _tree(op, xs):
    xs = list(xs)
    while len(xs) > 1:
        xs = [op(xs[i], xs[i + 1]) for i in range(0, len(xs) - 1, 2)] + ([xs[-1]] if len(xs) % 2 else [])
    return xs[0]


def _peer_select_kernel(h_ref, wqt_ref, keys_ref, gate_ref, i1_ref, i2_ref, s_sc, t_sc, i_sc, g_all, e1_all, e2_all):
    tb = h_ref.shape[0]
    kk = PEER_TOPK
    nk = PEER_N_KEYS
    nh = PEER_HEADS
    ninf = -jnp.inf
    qt = lax.dot_general(wqt_ref[...], h_ref[...], (((1,), (1,)), ((), ())), preferred_element_type=F32)
    qt = qt.astype(BF16)
    for half in range(2):
        for h in range(nh):
            r0 = h * PEER_QUERY_DIM + half * PEER_HALF
            s = jnp.dot(keys_ref[half * nh + h], qt[r0:r0 + PEER_HALF], preferred_element_type=F32)
            row0 = (half * nh + h) * SEL_PITCH
            for lt in range(tb // 128):
                s_sc[lt, row0:row0 + nk, :] = s[:, lt * 128:(lt + 1) * 128]

    sort16 = _bitonic_stages(kk)
    merge16 = [[(i, i ^ j, True) for i in range(kk) if (i ^ j) > i] for j in (8, 4, 2, 1)]

    def per_tile(lt, carry):
        def stage1(half, carry1):
            base = half * (nh * SEL_PITCH)
            top_k = top_i = None
            for g in range(nk // kk):
                ks, ix = [], []
                for j in range(kk):
                    key = g * kk + j
                    v = s_sc[lt, pl.ds(base + key, nh, stride=SEL_PITCH), :]
                    ks.append(_order_key(v + 0.0))
                    ix.append(jnp.full((nh, 128), key, I32))
                ks, ix = _exchange(ks, ix, sort16)
                if top_k is None:
                    top_k, top_i = ks, ix
                else:
                    ck, ci = [], []
                    for i in range(kk):
                        c = _ranks_before(top_k[i], top_i[i], ks[kk - 1 - i], ix[kk - 1 - i])
                        ck.append(jnp.where(c, top_k[i], ks[kk - 1 - i]))
                        ci.append(jnp.where(c, top_i[i], ix[kk - 1 - i]))
                    top_k, top_i = _exchange(ck, ci, merge16)
            for a in range(kk):
                t_sc[lt, half, a] = lax.bitcast_convert_type(_order_key(top_k[a]), F32)
                i_sc[lt, half, a] = top_i[a]
            return carry1

        lax.fori_loop(0, 2, stage1, 0)

        t1 = [t_sc[lt, 0, a] for a in range(kk)]
        t2 = [t_sc[lt, 1, a] for a in range(kk)]
        i1 = [i_sc[lt, 0, a] for a in range(kk)]
        i2 = [i_sc[lt, 1, a] for a in range(kk)]
        cand = [t1[a] + t2[0] for a in range(kk)]
        ptr = [jnp.zeros((nh, 128), I32) for _ in range(kk)]
        best, e1s, e2s = [], [], []
        for _ in range(kk):
            m = _tree(jnp.maximum, cand)
            astar = _tree(jnp.minimum, [jnp.where(cand[a] == m, a, kk) for a in range(kk)])
            wins = [astar == a for a in range(kk)]
            bstar = _tree(jnp.maximum, [jnp.where(wins[a], ptr[a], -1) for a in range(kk)])
            best.append(m)
            e1s.append(_tree(jnp.maximum, [jnp.where(wins[a], i1[a], -1) for a in range(kk)]))
            e2s.append(_tree(jnp.maximum, [jnp.where(bstar == b, i2[b], -1) for b in range(kk)]))
            nxt = _tree(jnp.maximum, [jnp.where(bstar + 1 == b, t2[b], ninf) for b in range(kk)])
            cand = [jnp.where(wins[a], t1[a] + nxt, cand[a]) for a in range(kk)]
            ptr = [jnp.where(wins[a], ptr[a] + 1, ptr[a]) for a in range(kk)]
        ex = [jnp.exp(v - best[0]) for v in best]
        den = _tree(jnp.add, ex)
        for it in range(kk):
            rows = pl.ds(it, nh, stride=kk)
            g_all[lt, rows, :] = ex[it] / den
            e1_all[lt, rows, :] = e1s[it]
            e2_all[lt, rows, :] = e2s[it]
        return carry

    lax.fori_loop(0, tb // 128, per_tile, 0)
    for lt in range(tb // 128):
        gate_ref[lt * 128:(lt + 1) * 128, :] = g_all[lt].T
        i1_ref[lt * 128:(lt + 1) * 128, :] = e1_all[lt].T
        i2_ref[lt * 128:(lt + 1) * 128, :] = e2_all[lt].T


def _peer_select(h2f, wqt_bf, keys_bf):
    nt, d = h2f.shape
    tb = TOKEN_BLOCK
    nlt = tb // 128
    r = PEER_HEADS * PEER_TOPK
    row = pl.BlockSpec((tb, r), lambda i: (i, 0))
    return pl.pallas_call(
        _peer_select_kernel,
        grid=(nt // tb,),
        in_specs=[pl.BlockSpec((tb, d), lambda i: (i, 0)),
                  pl.BlockSpec((d, d), lambda i: (0, 0)),
                  pl.BlockSpec((2 * PEER_HEADS, PEER_N_KEYS, PEER_HALF), lambda i: (0, 0, 0))],
        out_specs=[row, row, row],
        out_shape=[jax.ShapeDtypeStruct((nt, r), F32), jax.ShapeDtypeStruct((nt, r), I32),
                   jax.ShapeDtypeStruct((nt, r), I32)],
        scratch_shapes=[pltpu.VMEM((nlt, 2 * PEER_HEADS * SEL_PITCH, 128), F32),
                        pltpu.VMEM((nlt, 2, PEER_TOPK, PEER_HEADS, 128), F32),
                        pltpu.VMEM((nlt, 2, PEER_TOPK, PEER_HEADS, 128), I32),
                        pltpu.VMEM((nlt, r, 128), F32),
                        pltpu.VMEM((nlt, r, 128), I32),
                        pltpu.VMEM((nlt, r, 128), I32)],
        compiler_params=_cparams("arbitrary"),
        name="peer_select",
    )(h2f, wqt_bf, keys_bf)


def _peer_build(rows, dst, row0, count, base):
    gate_ref, i1_ref, i2_ref = rows
    nk = PEER_N_KEYS
    siota = lax.broadcasted_iota(I32, (nk, nk), 0)
    for g8 in range(count // 8):
        r0 = row0 + g8 * 8
        if not isinstance(r0, int):
            r0 = pl.multiple_of(r0, 8)
        i1t = i1_ref[pl.ds(r0, 8), :]
        i2t = i2_ref[pl.ds(r0, 8), :]
        gt = gate_ref[pl.ds(r0, 8), :]
        for k in range(8):
            pt = jnp.where(siota == i1t[k:k + 1], gt[k:k + 1], 0.0).astype(BF16)
            qt = jnp.where(siota == i2t[k:k + 1], 1.0, 0.0)
            g = jnp.dot(pt, qt.T.astype(BF16), preferred_element_type=F32)
            gb = lax.bitcast_convert_type(g.astype(BF16).astype(F32), jnp.uint32)
            pk = (gb[:nk // 2] >> 16) | gb[nk // 2:]
            dst[pl.ds(pl.multiple_of(base + (r0 + k) * PEER_PITCH, 8), nk // 2), :] = pk


def _peer_kernel(rows0_g, rows0_1, rows0_2, rows_g, rows_1, rows_2, h_ref, ua_ref, ub_ref, v_ref, x_ref, mod_ref,
                 o_ref,
                 gw, wsc, acc, *, nblocks, ntiles, blocks_per_batch, lc):
    tb = h_ref.shape[0]
    nk = PEER_N_KEYS
    half = PEER_EXPERT_TILE // 2
    per_step = tb // ntiles
    buf_rows = tb * PEER_PITCH
    g = pl.program_id(0)
    blk = jnp.minimum(g // ntiles, nblocks - 1)
    tile = g % ntiles
    ptile = (g + ntiles - 1) % ntiles

    @pl.when(g == 0)
    def _():
        def first(t8, carry):
            _peer_build((rows0_g, rows0_1, rows0_2), gw, t8 * 8, 8, 0)
            return carry
        lax.fori_loop(0, tb // 8, first, 0)
        wsc[1] = jnp.zeros(wsc.shape[1:], wsc.dtype)
        acc[...] = jnp.zeros_like(acc)

    _peer_build((rows_g, rows_1, rows_2), gw, tile * per_step, per_step, ((blk + 1) % 2) * buf_rows)

    h = h_ref[...]
    slot = g % 2
    base = (blk % 2) * buf_rows + tile * (half // nk)
    for s in range(half // nk):
        uc = jnp.concatenate([ua_ref[:, s * nk:(s + 1) * nk], ub_ref[:, s * nk:(s + 1) * nk]], axis=1)
        a = jnp.dot(h, uc, preferred_element_type=F32)
        pk = gw[pl.ds(base + s, tb, stride=PEER_PITCH), :]
        gs = jnp.concatenate([lax.bitcast_convert_type(pk << 16, F32),
                              lax.bitcast_convert_type(pk & jnp.uint32(0xFFFF0000), F32)], axis=1)
        w = (gs * jax.nn.gelu(a)).astype(BF16)
        wsc[slot, :, s * nk:(s + 1) * nk] = w[:, :nk]
        wsc[slot, :, half + s * nk:half + (s + 1) * nk] = w[:, nk:]

    r = jnp.dot(wsc[1 - slot], v_ref[...].reshape(PEER_EXPERT_TILE, v_ref.shape[-1]), preferred_element_type=F32)
    acc[...] = acc[...] * jnp.where(ptile == 0, 0.0, 1.0) + r

    @pl.when((ptile == ntiles - 1) & (g > 0))
    def _():
        pb = (g - 1) // ntiles
        nctx = jnp.clip(lc - (pb % blocks_per_batch) * tb, 0, tb)
        rowi = lax.broadcasted_iota(I32, (tb, 1), 0)
        gate = jnp.where(rowi < nctx, mod_ref[0, 0][5:6], mod_ref[0, 1][5:6])
        o_ref[...] = x_ref[...] + gate * acc[...]


def _peer(gate, i1, i2, h2f, ut_bf, v_bf, xf, modp, n, lc):
    nt, d = h2f.shape
    tb = PEER_TOKEN_BLOCK if n % PEER_TOKEN_BLOCK == 0 else TOKEN_BLOCK
    r = PEER_HEADS * PEER_TOPK
    te = PEER_EXPERT_TILE
    nblocks = nt // tb
    ntiles = PEER_EXPERTS // te
    bpb = n // tb
    v3 = v_bf.reshape(2, PEER_EXPERTS // 2, d)
    hb = (PEER_EXPERTS // 2) // (te // 2)

    def pblk(g):
        return jnp.maximum(g - 1, 0) // ntiles

    row0 = pl.BlockSpec((tb, r), lambda g: (0, 0))
    rown = pl.BlockSpec((tb, r), lambda g: (jnp.minimum(g // ntiles + 1, nblocks - 1), 0))
    return pl.pallas_call(
        functools.partial(_peer_kernel, nblocks=nblocks, ntiles=ntiles, blocks_per_batch=bpb, lc=lc),
        grid=(nblocks * ntiles + 1,),
        in_specs=[row0, row0, row0, rown, rown, rown,
                  pl.BlockSpec((tb, d), lambda g: (jnp.minimum(g // ntiles, nblocks - 1), 0)),
                  pl.BlockSpec((d, te // 2), lambda g: (0, g % ntiles)),
                  pl.BlockSpec((d, te // 2), lambda g: (0, hb + g % ntiles)),
                  pl.BlockSpec((2, te // 2, d), lambda g: (0, (g + ntiles - 1) % ntiles, 0)),
                  pl.BlockSpec((tb, d), lambda g: (pblk(g), 0)),
                  pl.BlockSpec((1, 2, 8, d), lambda g: (pblk(g) // bpb, 0, 0, 0))],
        out_specs=pl.BlockSpec((tb, d), lambda g: (pblk(g), 0)),
        out_shape=jax.ShapeDtypeStruct((nt, d), F32),
        scratch_shapes=[pltpu.VMEM((2 * tb * PEER_PITCH, PEER_N_KEYS), jnp.uint32),
                        pltpu.VMEM((2, tb, te), BF16),
                        pltpu.VMEM((tb, d), F32)],
        compiler_params=_cparams("arbitrary"),
        name="peer",
    )(gate, i1, i2, gate, i1, i2, h2f, ut_bf, ut_bf, v3, xf, modp)


def _rope_tables(lc, l):
    t = jnp.arange(l)
    r = (t // GRID_W).astype(F32)
    col = (t % GRID_W).astype(F32)
    pairs = HEAD_DIM // 4
    inv = ROPE_BASE ** (-jnp.arange(pairs, dtype=F32) / pairs)
    ang_a = jnp.concatenate([r[:, None] * inv, col[:, None] * inv], axis=-1)
    half = RET_HEAD_DIM // 2
    inv_r = ROPE_BASE ** (-jnp.arange(half, dtype=F32) / half)
    ang_r = jnp.arange(l, dtype=F32)[:, None] * inv_r

    def tables(ang, heads):
        cos, sin = jnp.cos(ang), jnp.sin(ang)
        zero = jnp.zeros_like(sin)
        c = jnp.concatenate([cos, cos], -1)
        s1 = jnp.concatenate([-sin, zero], -1)
        s2 = jnp.concatenate([zero, sin], -1)
        w = c.shape[-1]
        ident = (jnp.ones((lc, w), F32), jnp.zeros((lc, w), F32), jnp.zeros((lc, w), F32))
        return tuple(jnp.tile(jnp.concatenate([i, x], 0), (1, heads)) for i, x in zip(ident, (c, s1, s2)))

    return tables(ang_a, N_HEADS) + tables(ang_r, RET_HEADS)


def kernel(x, c, ctx, c_ctx, ada_w, ada_b, norm1_w, norm2_w, w_in, w_out, q_norm_w, k_norm_w, attn_sink,
           s5_a_re, s5_a_im, s5_log_dt, s5_b_re, s5_b_im, s5_c_re, s5_c_im, s5_d, s5_glu_w, s5_glu_b,
           ret_decay, peer_wq, peer_keys, peer_u, peer_v):
    b, l, d = x.shape
    lc = ctx.shape[1]
    n = lc + l
    depth = ada_w.shape[0]
    tm = TOKEN_BLOCK
    ncb = lc // tm
    bpb = n // tm

    xa = jnp.concatenate([ctx, x], axis=1)
    rope_tabs = _rope_tables(lc, l)
    head_of = jnp.arange(ATT_Q_W) // HEAD_DIM
    bd = (head_of[:, None] == head_of[None, :]).astype(F32)
    crows = -(-(b + 1) // 8) * 8
    cpad = jnp.concatenate([c, c_ctx[None], jnp.zeros((crows - b - 1, d), F32)], axis=0)

    for layer in range(depth):
        mods = _adaln(cpad, ada_w[layer], ada_b[layer])
        mod = mods[:b].reshape(b, N_MOD, d)
        mod_c = jnp.broadcast_to(mods[b].reshape(1, N_MOD, d), (b, N_MOD, d))
        modp = jnp.pad(jnp.stack([mod_c, mod], axis=1), ((0, 0), (0, 0), (0, 8 - N_MOD), (0, 0)))

        z = _proj(xa, modp, norm1_w[layer], w_in[layer].astype(BF16), ncb)
        qnw = jnp.tile(q_norm_w[layer], N_HEADS).reshape(1, ATT_Q_W)
        knw = jnp.tile(k_norm_w[layer], N_KV_HEADS).reshape(1, ATT_KV_W)
        qa, ka, va, su, rq, rk, rv, rg, su2 = _prep(z, rope_tabs, qnw, knw, bd)

        att = _attention(attn_sink[layer].astype(F32), qa, ka, va, lc)
        mats = _s5_mats(s5_a_re[layer], s5_a_im[layer], s5_log_dt[layer], s5_b_re[layer], s5_b_im[layer],
                        s5_c_re[layer], s5_c_im[layer])
        ssm = _s5_group(su, su2, mats, s5_d[layer], s5_glu_w[layer].astype(BF16), s5_glu_b[layer], lc)
        ret = _retention(rq, rk, rv, rg, _ret_tables(ret_decay[layer]), lc)

        xm, h2 = _mix(xa, att, ssm, ret, w_out[layer].astype(BF16), modp, norm2_w[layer], ncb)

        h2f = h2.reshape(b * n, d)
        keys = peer_keys[layer].reshape(2 * PEER_HEADS, PEER_N_KEYS, PEER_HALF).astype(BF16)
        gate, i1, i2 = _peer_select(h2f, peer_wq[layer].T.astype(BF16), keys)
        xf = _peer(gate, i1, i2, h2f, peer_u[layer].T.astype(BF16), peer_v[layer].astype(BF16),
                   xm.reshape(b * n, d), modp, n, lc)
        xa = xf.reshape(b, n, d)

    return xa[:, lc:]
```
